```python
import math
import jax, jax.numpy as jnp
from jax import lax
import numpy as np

D_MODEL = 1024
BATCH = 4
SEQ = 8192
DEPTH = 2
DEC_BATCH = 32
DEC_SEQ = 4
PAST_LEN = 16384
PAGE_SIZE = 128

GROUP_WIDTH = D_MODEL // 4
HEAD_DIM = 64
N_HEADS = GROUP_WIDTH // HEAD_DIM
N_IN_SLICES = 11
N_KV_SLOTS = 6
CONV_WIDTH = 4
LRU_C = 8.0
MOBA_BLOCK = 256
MOBA_TOPK = 3
QUERY_BLOCK = 128
DIFF_HALF = HEAD_DIM // 2
N_BUCKETS = 32
MAX_EXACT = N_BUCKETS // 2
MAX_DISTANCE = 128
MEM_LEN = 256
MEM_HEADS = 4
MEM_HEAD_DIM = GROUP_WIDTH // MEM_HEADS
N_GROUPS = 4
EXPERTS_PER_GROUP = 4
N_EXPERTS = N_GROUPS * EXPERTS_PER_GROUP
TOP_K_INNER = 2
D_FF_EXPERT = D_MODEL // 2
DEEPNORM_ALPHA = (2 * DEPTH) ** 0.25
DEEPNORM_BETA = (8 * DEPTH) ** -0.25
LN_EPS = 1e-5
F32 = jnp.float32

kernel_name = 'hymba_style_rglru_moba_stick_diff_hmoe_step'


def layer_norm(x, g, b):
    xf = x.astype(F32)
    mu = xf.mean(-1, keepdims=True)
    var = jnp.mean(jnp.square(xf - mu), -1, keepdims=True)
    return ((xf - mu) * lax.rsqrt(var + LN_EPS) * g + b).astype(x.dtype)


def rel_bucket(dist):
    dist = jnp.maximum(dist, 0)
    large = MAX_EXACT + (jnp.log(jnp.maximum(dist, 1).astype(F32) / MAX_EXACT)
                         / math.log(MAX_DISTANCE / MAX_EXACT) * (N_BUCKETS - MAX_EXACT)).astype(jnp.int32)
    large = jnp.minimum(large, N_BUCKETS - 1)
    return jnp.where(dist < MAX_EXACT, dist, large)


def sweep_query_blocks(fn, q, q_pos):
    b, t = q.shape[:2]
    if t <= QUERY_BLOCK:
        return fn(q, q_pos)
    nb = -(-t // QUERY_BLOCK)
    pad = nb * QUERY_BLOCK - t
    qp = jnp.pad(q, ((0, 0), (0, pad)) + ((0, 0),) * (q.ndim - 2))
    pp = jnp.pad(q_pos, (0, pad), mode='edge')
    qb = jnp.swapaxes(qp.reshape((b, nb, QUERY_BLOCK) + q.shape[2:]), 0, 1)
    out = lax.map(lambda a: fn(a[0], a[1]), (qb, pp.reshape(nb, QUERY_BLOCK)))
    out = jnp.swapaxes(out, 0, 1)
    return out.reshape((b, nb * QUERY_BLOCK) + out.shape[3:])[:, :t]


def rglru_mixer(x_br, g_br, conv_buf, h0, conv_w, conv_b, wa, ba, wx, bx, lam):
    b, t, w = x_br.shape
    xc = jnp.concatenate([conv_buf.astype(x_br.dtype), x_br], axis=1)
    u = conv_b + sum(conv_w[k] * xc[:, k:k + t] for k in range(CONV_WIDTH))
    new_buf = xc[:, t:]
    ub = u.reshape(b, t, N_HEADS, w // N_HEADS)
    r = jax.nn.sigmoid((jnp.einsum('bthi,hij->bthj', ub, wa).reshape(b, t, w) + ba).astype(F32))
    i = jax.nn.sigmoid((jnp.einsum('bthi,hij->bthj', ub, wx).reshape(b, t, w) + bx).astype(F32))
    log_a = -LRU_C * r * jax.nn.softplus(-lam.astype(F32))
    a = jnp.exp(log_a)
    inp = jnp.sqrt(-jnp.expm1(2.0 * log_a)) * i * u.astype(F32)

    def combine(e, l):
        return (e[0] * l[0], l[0] * e[1] + l[1])

    a_cum, h = lax.associative_scan(combine, (a, inp), axis=1)
    h = h + a_cum * h0.astype(F32)[:, None]
    y = h * jax.nn.gelu(g_br.astype(F32))
    return y.astype(x_br.dtype), new_buf, h[:, -1].astype(x_br.dtype)


def moba_mixer(q, q_pos, k, v, tab):
    b, tk = k.shape[:2]
    nb = -(-tk // MOBA_BLOCK)
    pad = nb * MOBA_BLOCK - tk

    def to_blocks(t):
        t = jnp.pad(t.astype(F32), ((0, 0), (0, pad), (0, 0), (0, 0)))
        return t.reshape(b, nb, MOBA_BLOCK, N_HEADS, HEAD_DIM).transpose(0, 3, 1, 2, 4)

    kb, vb = to_blocks(k), to_blocks(v)
    k_mean = kb.mean(axis=3)
    top_n = min(MOBA_TOPK, nb)
    blk_ids = jnp.arange(nb)
    offs = jnp.arange(MOBA_BLOCK)
    head_ids = jnp.arange(N_HEADS)[None, :, None, None, None]
    tab_t = tab.T.astype(F32)
    gather = jax.vmap(jax.vmap(lambda blocks, ids: blocks[ids]))

    def block(qb, pb):
        qf = qb.astype(F32)
        q_blk = pb // MOBA_BLOCK
        gate = jnp.einsum('bqhd,bhnd->bhqn', qf, k_mean)
        past = blk_ids[None, :] < q_blk[:, None]
        _, top = lax.top_k(jnp.where(past, gate, -jnp.inf), top_n)
        own = jnp.broadcast_to(q_blk[None, None, :, None], top.shape[:3] + (1,))
        ids = jnp.concatenate([top, own], axis=-1)
        kg, vg = gather(kb, ids), gather(vb, ids)
        k_pos = ids[..., None] * MOBA_BLOCK + offs
        qp = pb[None, None, :, None, None]
        sel_ok = jnp.broadcast_to((top < q_blk[None, None, :, None])[..., None], top.shape + (MOBA_BLOCK,))
        own_ok = k_pos[..., -1:, :] <= qp
        ok = jnp.concatenate([sel_ok, own_ok], axis=3)
        s = jnp.einsum('bqhd,bhqnpd->bhqnp', qf, kg) * HEAD_DIM ** -0.5 + tab_t[head_ids, rel_bucket(qp - k_pos)]
        s = jnp.where(ok, s, -jnp.inf)
        p = jax.nn.softmax(s.reshape(s.shape[:3] + (-1,)), axis=-1).reshape(s.shape)
        return jnp.einsum('bhqnp,bhqnpd->bqhd', p, vg).astype(qb.dtype)

    return sweep_query_blocks(block, q, q_pos)


def stick_mixer(q, q_pos, k, v):
    k_pos = jnp.arange(k.shape[1])
    kf, vf = k.astype(F32), v.astype(F32)

    def block(qb, pb):
        z = jnp.einsum('bqhd,bkhd->bhqk', qb.astype(F32), kf) * HEAD_DIM ** -0.5
        mask = k_pos[None, :] < pb[:, None]
        log_keep = jnp.where(mask, jax.nn.log_sigmoid(-z), 0.0)
        log_after = lax.cumsum(log_keep, axis=3, reverse=True) - log_keep
        att = jnp.where(mask, jnp.exp(jax.nn.log_sigmoid(z) + log_after), 0.0)
        return jnp.einsum('bhqk,bkhd->bqhd', att, vf).astype(qb.dtype)

    return sweep_query_blocks(block, q, q_pos)


def diff_mixer(q, q_pos, k, v, lam, lam_init, subln_w, tab):
    k_pos = jnp.arange(k.shape[1])
    kf, vf = k.astype(F32), v.astype(F32)
    k1, k2 = kf[..., :DIFF_HALF], kf[..., DIFF_HALF:]
    tab = tab.astype(F32)

    def block(qb, pb):
        qf = qb.astype(F32)
        bias = tab[rel_bucket(pb[:, None] - k_pos[None, :])].transpose(2, 0, 1)
        mask = k_pos[None, :] <= pb[:, None]

        def probs(qh, kh):
            s = jnp.einsum('bqhd,bkhd->bhqk', qh, kh) * DIFF_HALF ** -0.5 + bias
            return jax.nn.softmax(jnp.where(mask, s, -jnp.inf), axis=-1)

        w = probs(qf[..., :DIFF_HALF], k1) - lam * probs(qf[..., DIFF_HALF:], k2)
        o = jnp.einsum('bhqk,bkhd->bqhd', w, vf)
        o = o * lax.rsqrt(jnp.mean(o * o, -1, keepdims=True) + LN_EPS) * subln_w * (1.0 - lam_init)
        return o.astype(qb.dtype)

    return sweep_query_blocks(block, q, q_pos)


def memory_kv(mem, w_xkv):
    b, m, _ = mem.shape
    return (mem @ w_xkv).reshape(b, m, 2, MEM_HEADS, MEM_HEAD_DIM)


def cross_attend(x, mem_kv, w_xq, w_xo):
    b, t, _ = x.shape
    q = (x @ w_xq).reshape(b, t, MEM_HEADS, MEM_HEAD_DIM).astype(F32)
    s = jnp.einsum('bthd,bmhd->bhtm', q, mem_kv[:, :, 0].astype(F32)) * MEM_HEAD_DIM ** -0.5
    p = jax.nn.softmax(s, axis=-1)
    o = jnp.einsum('bhtm,bmhd->bthd', p, mem_kv[:, :, 1].astype(F32)).reshape(b, t, GROUP_WIDTH)
    return o.astype(x.dtype) @ w_xo


def hier_moe(x, w_rg, b_rg, w_re, b_re, w_gate, w_up, w_down):
    b, t, d = x.shape
    xf = x.reshape(-1, d)
    n = xf.shape[0]
    pg = jax.nn.softmax((xf @ w_rg).astype(F32) + b_rg, axis=-1)
    g = jnp.argmax(pg, axis=-1)
    pg_sel = jnp.take_along_axis(pg, g[:, None], axis=1)
    le = ((xf @ w_re).astype(F32) + b_re).reshape(n, N_GROUPS, EXPERTS_PER_GROUP)
    le = jnp.take_along_axis(le, g[:, None, None], axis=1)[:, 0]
    top_p, top_i = lax.top_k(jax.nn.softmax(le, axis=-1), TOP_K_INNER)
    gates = pg_sel * top_p / jnp.sum(top_p, axis=-1, keepdims=True)
    expert_ids = g[:, None] * EXPERTS_PER_GROUP + top_i
    combine = jnp.sum(jax.nn.one_hot(expert_ids, N_EXPERTS, dtype=F32) * gates[..., None], axis=1)
    y = jnp.zeros((n, d), F32)
    for e in range(N_EXPERTS):
        h = jax.nn.silu(xf @ w_gate[e]) * (xf @ w_up[e])
        y = y + combine[:, e:e + 1] * (h @ w_down[e]).astype(F32)
    return y.reshape(b, t, d).astype(x.dtype)


def trunk_layer(x, l, q_pos, past, conv_buf, h0, mem_kv, p):
    b, t, _ = x.shape
    proj = x @ p['w_in'][l]
    a_x, a_g, bq, bk, bv, cq, ck, cv, dq, dk, dv = jnp.split(proj, N_IN_SLICES, axis=-1)

    def heads(z):
        return z.reshape(b, t, N_HEADS, HEAD_DIM)

    new_kv = jnp.stack([heads(z) for z in (bk, bv, ck, cv, dk, dv)], axis=2)

    def full_kv(m):
        k_new, v_new = new_kv[:, :, 2 * m], new_kv[:, :, 2 * m + 1]
        if past is None:
            return k_new, v_new
        return (jnp.concatenate([past[m][:, :, 0].astype(x.dtype), k_new], axis=1),
                jnp.concatenate([past[m][:, :, 1].astype(x.dtype), v_new], axis=1))

    y_a, conv_new, h_new = rglru_mixer(a_x, a_g, conv_buf, h0, p['conv_w'][l], p['conv_b'][l],
                                       p['lru_wa'][l], p['lru_ba'][l], p['lru_wx'][l], p['lru_bx'][l],
                                       p['lru_lambda'][l])
    k_b, v_b = full_kv(0)
    y_b = moba_mixer(heads(bq), q_pos, k_b, v_b, p['rel_bias'][:, :N_HEADS])
    k_c, v_c = full_kv(1)
    y_c = stick_mixer(heads(cq), q_pos, k_c, v_c)
    k_d, v_d = full_kv(2)
    dl = p['diff_lambda'][l].astype(F32)
    lam_init = 0.8 - 0.6 * math.exp(-0.3 * l)
    lam = jnp.exp(jnp.sum(dl[0] * dl[1])) - jnp.exp(jnp.sum(dl[2] * dl[3])) + lam_init
    y_d = diff_mixer(heads(dq), q_pos, k_d, v_d, lam, lam_init, p['diff_subln'][l], p['rel_bias'][:, N_HEADS:])
    mixed = jnp.concatenate([y_a, y_b.reshape(b, t, GROUP_WIDTH), y_c.reshape(b, t, GROUP_WIDTH),
                             y_d.reshape(b, t, GROUP_WIDTH)], axis=-1) @ p['w_out'][l]
    x = layer_norm(DEEPNORM_ALPHA * x + mixed, p['ln1_g'][l], p['ln1_b'][l])
    x = layer_norm(DEEPNORM_ALPHA * x + cross_attend(x, mem_kv, p['w_xq'][l], p['w_xo'][l]),
                   p['ln2_g'][l], p['ln2_b'][l])
    x = layer_norm(DEEPNORM_ALPHA * x + hier_moe(x, p['w_rg'][l], p['b_rg'][l], p['w_re'][l], p['b_re'][l],
                                                  p['w_gate'][l], p['w_up'][l], p['w_down'][l]),
                   p['ln3_g'][l], p['ln3_b'][l])
    return x, new_kv, conv_new, h_new


def setup_inputs(seed: int = 0) -> dict:
    key = jax.random.key(seed)
    ks = iter(jax.random.split(key, 40))

    def nrm(shape, scale):
        return jax.random.normal(next(ks), shape, F32) * scale

    n_pages = PAST_LEN // PAGE_SIZE
    n_used = DEC_BATCH * n_pages
    n_pool = n_used + n_used // 4
    d_in = N_IN_SLICES * GROUP_WIDTH
    x_prompt = nrm((BATCH, SEQ, D_MODEL), 1.0)
    x_sample = nrm((DEC_BATCH, DEC_SEQ, D_MODEL), 1.0)
    cache_kv = nrm((n_pool, DEPTH, PAGE_SIZE, N_KV_SLOTS, N_HEADS, HEAD_DIM), 1.0)
    cache_mem_kv = nrm((DEC_BATCH, DEPTH, MEM_LEN, 2, MEM_HEADS, MEM_HEAD_DIM), 1.0)
    state_rglru_h = nrm((DEC_BATCH, DEPTH, GROUP_WIDTH), 0.5)
    state_conv = nrm((DEC_BATCH, DEPTH, CONV_WIDTH - 1, GROUP_WIDTH), 1.0)
    page_table = jax.random.permutation(next(ks), n_pool)[:n_used].reshape(DEC_BATCH, n_pages).astype(jnp.int32)
    mem_prompt = nrm((BATCH, MEM_LEN, D_MODEL), 1.0)
    rel_bias = nrm((N_BUCKETS, 2 * N_HEADS), 0.1)
    w_in = nrm((DEPTH, D_MODEL, d_in), D_MODEL ** -0.5)
    conv_w = nrm((DEPTH, CONV_WIDTH, GROUP_WIDTH), CONV_WIDTH ** -0.5)
    conv_b = nrm((DEPTH, GROUP_WIDTH), 0.01)
    lru_wa = nrm((DEPTH, N_HEADS, HEAD_DIM, HEAD_DIM), HEAD_DIM ** -0.5)
    lru_ba = nrm((DEPTH, GROUP_WIDTH), 0.01)
    lru_wx = nrm((DEPTH, N_HEADS, HEAD_DIM, HEAD_DIM), HEAD_DIM ** -0.5)
    lru_bx = nrm((DEPTH, GROUP_WIDTH), 0.01)
    a0 = jax.random.uniform(next(ks), (DEPTH, GROUP_WIDTH), F32, 0.9, 0.999)
    s = a0 ** (1.0 / LRU_C)
    lru_lambda = jnp.log(s) - jnp.log1p(-s)
    diff_lambda = nrm((DEPTH, 4, DIFF_HALF), 0.1)
    diff_subln = 1.0 + nrm((DEPTH, HEAD_DIM), 0.02)
    w_out = nrm((DEPTH, 4 * GROUP_WIDTH, D_MODEL), (4 * GROUP_WIDTH) ** -0.5 * DEEPNORM_BETA)
    ln1_g = 1.0 + nrm((DEPTH, D_MODEL), 0.02)
    ln1_b = nrm((DEPTH, D_MODEL), 0.02)
    w_xq = nrm((DEPTH, D_MODEL, GROUP_WIDTH), D_MODEL ** -0.5)
    w_xkv = nrm((DEPTH, D_MODEL, 2 * GROUP_WIDTH), D_MODEL ** -0.5)
    w_xo = nrm((DEPTH, GROUP_WIDTH, D_MODEL), GROUP_WIDTH ** -0.5 * DEEPNORM_BETA)
    ln2_g = 1.0 + nrm((DEPTH, D_MODEL), 0.02)
    ln2_b = nrm((DEPTH, D_MODEL), 0.02)
    w_rg = nrm((DEPTH, D_MODEL, N_GROUPS), D_MODEL ** -0.5)
    b_rg = nrm((DEPTH, N_GROUPS), 0.01)
    w_re = nrm((DEPTH, D_MODEL, N_EXPERTS), D_MODEL ** -0.5)
    b_re = nrm((DEPTH, N_EXPERTS), 0.01)
    w_gate = nrm((DEPTH, N_EXPERTS, D_MODEL, D_FF_EXPERT), D_MODEL ** -0.5)
    w_up = nrm((DEPTH, N_EXPERTS, D_MODEL, D_FF_EXPERT), D_MODEL ** -0.5)
    w_down = nrm((DEPTH, N_EXPERTS, D_FF_EXPERT, D_MODEL), D_FF_EXPERT ** -0.5 * DEEPNORM_BETA)
    ln3_g = 1.0 + nrm((DEPTH, D_MODEL), 0.02)
    ln3_b = nrm((DEPTH, D_MODEL), 0.02)
    return {'x_prompt': x_prompt, 'x_sample': x_sample, 'cache_kv': cache_kv,
            'cache_mem_kv': cache_mem_kv, 'state_rglru_h': state_rglru_h, 'state_conv': state_conv,
            'page_table': page_table, 'mem_prompt': mem_prompt, 'rel_bias': rel_bias,
            'w_in': w_in, 'conv_w': conv_w, 'conv_b': conv_b, 'lru_wa': lru_wa, 'lru_ba': lru_ba,
            'lru_wx': lru_wx, 'lru_bx': lru_bx, 'lru_lambda': lru_lambda, 'diff_lambda': diff_lambda,
            'diff_subln': diff_subln, 'w_out': w_out, 'ln1_g': ln1_g, 'ln1_b': ln1_b,
            'w_xq': w_xq, 'w_xkv': w_xkv, 'w_xo': w_xo, 'ln2_g': ln2_g, 'ln2_b': ln2_b,
            'w_rg': w_rg, 'b_rg': b_rg, 'w_re': w_re, 'b_re': b_re, 'w_gate': w_gate,
            'w_up': w_up, 'w_down': w_down, 'ln3_g': ln3_g, 'ln3_b': ln3_b}


def reference(x_prompt, x_sample, cache_kv, cache_mem_kv, state_rglru_h, state_conv, page_table,
              mem_prompt, rel_bias, w_in, conv_w, conv_b, lru_wa, lru_ba, lru_wx, lru_bx, lru_lambda,
              diff_lambda, diff_subln, w_out, ln1_g, ln1_b, w_xq, w_xkv, w_xo, ln2_g, ln2_b,
              w_rg, b_rg, w_re, b_re, w_gate, w_up, w_down, ln3_g, ln3_b):
    p = dict(rel_bias=rel_bias, w_in=w_in, conv_w=conv_w, conv_b=conv_b, lru_wa=lru_wa, lru_ba=lru_ba,
             lru_wx=lru_wx, lru_bx=lru_bx, lru_lambda=lru_lambda, diff_lambda=diff_lambda,
             diff_subln=diff_subln, w_out=w_out, ln1_g=ln1_g, ln1_b=ln1_b, w_xq=w_xq, w_xo=w_xo,
             ln2_g=ln2_g, ln2_b=ln2_b, w_rg=w_rg, b_rg=b_rg, w_re=w_re, b_re=b_re, w_gate=w_gate,
             w_up=w_up, w_down=w_down, ln3_g=ln3_g, ln3_b=ln3_b)
    bp, tp = x_prompt.shape[:2]
    bs, ts = x_sample.shape[:2]
    n_pages = PAST_LEN // PAGE_SIZE
    past_len = n_pages * PAGE_SIZE
    pos_prompt = jnp.arange(tp, dtype=jnp.int32)
    pos_sample = past_len + jnp.arange(ts, dtype=jnp.int32)
    zero_conv = jnp.zeros((bp, CONV_WIDTH - 1, GROUP_WIDTH), x_prompt.dtype)
    zero_h = jnp.zeros((bp, GROUP_WIDTH), x_prompt.dtype)
    xp, xs = x_prompt, x_sample
    kv_p, mem_p, h_p, conv_p, kv_s, h_s, conv_s = [], [], [], [], [], [], []
    for l in range(DEPTH):
        mkv = memory_kv(mem_prompt, w_xkv[l])
        xp, kv_new, conv_new, h_new = trunk_layer(xp, l, pos_prompt, None, zero_conv, zero_h, mkv, p)
        kv_p.append(kv_new); mem_p.append(mkv); h_p.append(h_new); conv_p.append(conv_new)
        past = tuple(cache_kv[page_table, l, :, 2 * m:2 * m + 2].reshape(bs, past_len, 2, N_HEADS, HEAD_DIM)
                     for m in range(3))
        xs, kv_new, conv_new, h_new = trunk_layer(xs, l, pos_sample, past, state_conv[:, l],
                                                  state_rglru_h[:, l], cache_mem_kv[:, l], p)
        kv_s.append(kv_new); h_s.append(h_new); conv_s.append(conv_new)
    y_prompt, y_sample = xp, xs
    kv_prompt = jnp.stack(kv_p, axis=1)
    mem_kv_prompt = jnp.stack(mem_p, axis=1)
    h_prompt = jnp.stack(h_p, axis=1)
    conv_prompt = jnp.stack(conv_p, axis=1)
    kv_sample = jnp.stack(kv_s, axis=1)
    h_sample = jnp.stack(h_s, axis=1)
    conv_sample = jnp.stack(conv_s, axis=1)
    return (y_prompt, y_sample, kv_prompt, mem_kv_prompt, h_prompt, conv_prompt, kv_sample, h_sample, conv_sample)
```

```python
import functools
import math

import numpy as np
import jax
import jax.numpy as jnp
from jax import lax
from jax.experimental import pallas as pl
from jax.experimental.pallas import tpu as pltpu

F32 = jnp.float32
BF16 = jnp.bfloat16

HEAD_DIM = 64
N_HEADS = 4
GROUP_WIDTH = N_HEADS * HEAD_DIM
CONV_WIDTH = 4
LRU_C = 8.0
MOBA_BLOCK = 256
MOBA_TOPK = 3
DIFF_HALF = HEAD_DIM // 2
N_BUCKETS = 32
MAX_EXACT = N_BUCKETS // 2
MAX_DISTANCE = 128
N_GROUPS = 4
EXPERTS_PER_GROUP = 4
N_EXPERTS = N_GROUPS * EXPERTS_PER_GROUP
LN_EPS = 1e-5
NEG = -1e30
LANES = 128
SUBLANES = 8
VMEM_LIMIT = 56 * 1024 * 1024
TQ = MOBA_BLOCK
SAMPLE_ROWS = SUBLANES
PAGES_PER_STEP = 8


def _cparams(*sem):
    return pltpu.CompilerParams(dimension_semantics=sem, vmem_limit_bytes=VMEM_LIMIT)


def _softplus(x):
    return jnp.maximum(x, 0.0) + jnp.log1p(jnp.exp(-jnp.abs(x)))


def _sigmoid(x):
    return 1.0 / (1.0 + jnp.exp(-x))


def _dot(a, b):
    return jnp.dot(a, b, preferred_element_type=F32)


def _dot_nt(a, b, precision=None):
    return lax.dot_general(a, b, (((1,), (1,)), ((), ())), preferred_element_type=F32, precision=precision)


def _layer_norm(x, g, b):
    mu = jnp.mean(x, axis=-1, keepdims=True)
    xc = x - mu
    var = jnp.mean(xc * xc, axis=-1, keepdims=True)
    return xc * lax.rsqrt(var + LN_EPS) * g + b


def _bucket_table(n):
    d = np.arange(n)
    large = MAX_EXACT + (np.log(np.maximum(d, 1).astype(np.float32) / MAX_EXACT)
                         / math.log(MAX_DISTANCE / MAX_EXACT) * (N_BUCKETS - MAX_EXACT)).astype(np.int32)
    return np.where(d < MAX_EXACT, d, np.minimum(large, N_BUCKETS - 1)).astype(np.int32)


def _mm_kernel(*refs, n_out):
    x = refs[0][...].astype(BF16)
    for w_ref, o_ref in zip(refs[1:1 + n_out], refs[1 + n_out:]):
        o_ref[...] = _dot(x, w_ref[...])


def _matmul(x, ws, tm=512):
    m, k = x.shape
    tm = min(tm, m)
    return pl.pallas_call(
        functools.partial(_mm_kernel, n_out=len(ws)),
        grid=(m // tm,),
        in_specs=[pl.BlockSpec((tm, k), lambda i: (i, 0))] + [pl.BlockSpec(w.shape, lambda i: (0, 0)) for w in ws],
        out_specs=[pl.BlockSpec((tm, w.shape[1]), lambda i: (i, 0)) for w in ws],
        out_shape=[jax.ShapeDtypeStruct((m, w.shape[1]), F32) for w in ws],
        compiler_params=_cparams("parallel"),
    )(x, *ws)


def _rglru_kernel(x_ref, g_ref, st_ref, pv_ref, w_ref, y_ref, so_ref, carry_ref, *, tc, tv):
    c = pl.program_id(1)

    @pl.when(c == 0)
    def _():
        carry_ref[...] = st_ref[0]

    w = GROUP_WIDTH
    x = x_ref[0]
    row = lax.broadcasted_iota(jnp.int32, (tc, w), 0)
    pv = pv_ref[...]
    conv_b, ba, bx, lam = pv[0:1], pv[1:2], pv[2:3], pv[3:4]
    prev = carry_ref[...]
    u = conv_b + pv[4 + CONV_WIDTH - 1:4 + CONV_WIDTH] * x
    for sft in range(1, CONV_WIDTH):
        xs = pltpu.roll(x, sft, 0)
        for i in range(sft):
            xs = jnp.where(row == i, prev[CONV_WIDTH - 1 + i - sft:CONV_WIDTH + i - sft], xs)
        u = u + pv[4 + CONV_WIDTH - 1 - sft:4 + CONV_WIDTH - sft] * xs
    gates = _dot(u.astype(BF16), w_ref[...])
    r = _sigmoid(gates[:, :w] + ba)
    ig = _sigmoid(gates[:, w:] + bx)
    log_a = -LRU_C * r * _softplus(-lam)
    a = jnp.exp(log_a)
    b = jnp.sqrt(-jnp.tanh(log_a) * (jnp.exp(2.0 * log_a) + 1.0)) * ig * u
    s = 1
    while s < tc:
        a_s = pltpu.roll(a, s, 0)
        b_s = pltpu.roll(b, s, 0)
        m = row >= s
        b = jnp.where(m, a * b_s + b, b)
        a = jnp.where(m, a * a_s, a)
        s *= 2
    h = b + a * prev[CONV_WIDTH - 1:CONV_WIDTH]
    gt = g_ref[0]
    gelu = 0.5 * gt * (1.0 + jnp.tanh(math.sqrt(2.0 / math.pi) * (gt + 0.044715 * (gt * gt * gt))))
    y_ref[0] = h * gelu
    carry_ref[0:CONV_WIDTH - 1] = x[tv - (CONV_WIDTH - 1):tv]
    carry_ref[CONV_WIDTH - 1:CONV_WIDTH] = h[tv - 1:tv]

    @pl.when(c == pl.num_programs(1) - 1)
    def _():
        so_ref[0] = carry_ref[...]


def _rglru(qag, state, pvec, w_gates, tc, tv):
    b, t, _ = qag.shape
    w = GROUP_WIDTH
    assert t % tc == 0 and tv >= CONV_WIDTH - 1
    return pl.pallas_call(
        functools.partial(_rglru_kernel, tc=tc, tv=tv),
        grid=(b, t // tc),
        in_specs=[pl.BlockSpec((1, tc, w), lambda i, c: (i, c, 0)),
                  pl.BlockSpec((1, tc, w), lambda i, c: (i, c, 1)),
                  pl.BlockSpec((1, SUBLANES, w), lambda i, c: (i, 0, 0)),
                  pl.BlockSpec((SUBLANES, w), lambda i, c: (0, 0)),
                  pl.BlockSpec((w, 2 * w), lambda i, c: (0, 0))],
        out_specs=[pl.BlockSpec((1, tc, w), lambda i, c: (i, c, 0)),
                   pl.BlockSpec((1, SUBLANES, w), lambda i, c: (i, 0, 0))],
        out_shape=[jax.ShapeDtypeStruct((b, t, w), F32), jax.ShapeDtypeStruct((b, SUBLANES, w), F32)],
        scratch_shapes=[pltpu.VMEM((SUBLANES, w), F32)],
        compiler_params=_cparams("parallel", "arbitrary"),
    )(qag, qag, state, pvec, w_gates)


def _tri_tables(nq):
    qs, ks = [], []
    for qi in range(nq):
        for ki in range(qi, -1, -1):
            qs.append(qi)
            ks.append(ki)
    return jnp.asarray(np.array(qs, np.int32)), jnp.asarray(np.array(ks, np.int32))


def _attn_specs(qcol, kcol, vcol):
    return [pl.BlockSpec((1, TQ, LANES), lambda b, p, t, qt, kt: (b, qt[t], qcol + p)),
            pl.BlockSpec((1, TQ, LANES), lambda b, p, t, qt, kt: (b, kt[t], kcol + p)),
            pl.BlockSpec((1, TQ, LANES), lambda b, p, t, qt, kt: (b, kt[t], vcol + p))]


def _attn_out_spec():
    return pl.BlockSpec((1, TQ, LANES), lambda b, p, t, qt, kt: (b, qt[t], p))


def _online_softmax_step(s, vb, m_ref, l_ref, acc_ref, j):
    m_old = m_ref[j]
    m_new = jnp.maximum(m_old, jnp.max(s, axis=1, keepdims=True))
    alpha = jnp.exp(m_old - m_new)
    p = jnp.exp(s - m_new)
    l_ref[j] = alpha * l_ref[j] + jnp.sum(p, axis=1, keepdims=True)
    acc_ref[j] = alpha * acc_ref[j] + _dot(p.astype(BF16), vb)
    m_ref[j] = m_new


def _stick_kernel(qt_ref, kt_ref, q_ref, k_ref, v_ref, tri_ref, o_ref, qm_ref, acc_ref, run_ref):
    t = pl.program_id(2)
    qi, ki = qt_ref[t], kt_ref[t]
    head = lax.broadcasted_iota(jnp.int32, (TQ, LANES), 1) // HEAD_DIM

    @pl.when(ki == qi)
    def _():
        q = q_ref[0] * HEAD_DIM ** -0.5
        for h in range(2):
            qm_ref[h] = jnp.where(head == h, q, 0.0).astype(BF16)
        acc_ref[...] = jnp.zeros_like(acc_ref)
        run_ref[...] = jnp.zeros_like(run_ref)

    kb = k_ref[0].astype(BF16)
    vb = v_ref[0].astype(BF16)
    r = lax.broadcasted_iota(jnp.int32, (TQ, TQ), 0)
    c = lax.broadcasted_iota(jnp.int32, (TQ, TQ), 1)
    valid = jnp.logical_or(ki < qi, c < r)
    tri = tri_ref[...]
    for h in range(2):
        z = _dot_nt(qm_ref[h], kb)
        lk = jnp.where(valid, -_softplus(z), 0.0)
        hi = lk.astype(BF16)
        lo = (lk - hi.astype(F32)).astype(BF16)
        cum = _dot(hi, tri) + _dot(lo, tri)
        e = jnp.where(valid, jnp.exp(z + cum + run_ref[h]), 0.0)
        pv = _dot(e.astype(BF16), vb)
        acc_ref[...] += jnp.where(head == h, pv, 0.0)
        run_ref[h] += cum[:, 0:1]

    @pl.when(ki == 0)
    def _():
        o_ref[0] = acc_ref[...]


def _suffix_tri(n):
    i = np.arange(n)
    return jnp.asarray((i[:, None] >= i[None, :]).astype(np.float32), dtype=BF16)


def _stick_prompt(qag, kv):
    b, t, _ = qag.shape
    qt, kt = _tri_tables(t // TQ)
    return pl.pallas_call(
        _stick_kernel,
        grid_spec=pltpu.PrefetchScalarGridSpec(
            num_scalar_prefetch=2, grid=(b, 2, qt.shape[0]),
            in_specs=_attn_specs(6, 4, 6) + [pl.BlockSpec((TQ, TQ), lambda b, p, t, qt, kt: (0, 0))],
            out_specs=_attn_out_spec(),
            scratch_shapes=[pltpu.VMEM((2, TQ, LANES), BF16), pltpu.VMEM((TQ, LANES), F32),
                            pltpu.VMEM((2, TQ, 1), F32)]),
        out_shape=jax.ShapeDtypeStruct((b, t, GROUP_WIDTH), F32),
        compiler_params=_cparams("parallel", "parallel", "arbitrary"),
    )(qt, kt, qag, kv, kv, _suffix_tri(TQ))


def _moba_kernel(qt_ref, kt_ref, q_ref, k_ref, v_ref, km_ref, bias_ref, o_ref,
                 qm_ref, sel_ref, m_ref, l_ref, acc_ref):
    t = pl.program_id(2)
    qi, ki = qt_ref[t], kt_ref[t]
    nb = km_ref.shape[1]
    head = lax.broadcasted_iota(jnp.int32, (TQ, LANES), 1) // HEAD_DIM
    blk = lax.broadcasted_iota(jnp.int32, (TQ, nb), 1)
    blkf = blk.astype(F32)

    @pl.when(ki == qi)
    def _():
        q = q_ref[0]
        for h in range(2):
            qh = jnp.where(head == h, q, 0.0)
            gate = _dot_nt(qh, km_ref[0], precision=lax.Precision.HIGHEST)
            g = jnp.where(blk < qi, gate, -jnp.inf)
            sel = jnp.zeros((TQ, nb), F32)
            for _ in range(MOBA_TOPK):
                mx = jnp.max(g, axis=1, keepdims=True)
                idx = jnp.min(jnp.where(g == mx, blkf, float(nb)), axis=1, keepdims=True)
                hit = blkf == idx
                sel = jnp.where(hit, 1.0, sel)
                g = jnp.where(hit, -jnp.inf, g)
            sel_ref[h] = jnp.where(blk < qi, sel, 0.0)
            qm_ref[h] = (qh * HEAD_DIM ** -0.5).astype(BF16)
        m_ref[...] = jnp.full_like(m_ref, NEG)
        l_ref[...] = jnp.zeros_like(l_ref)
        acc_ref[...] = jnp.zeros_like(acc_ref)

    kb = k_ref[0].astype(BF16)
    vb = v_ref[0].astype(BF16)
    r = lax.broadcasted_iota(jnp.int32, (TQ, TQ), 0)
    c = lax.broadcasted_iota(jnp.int32, (TQ, TQ), 1)
    causal_pen = jnp.where(c <= r, 0.0, NEG)
    for h in range(2):
        picked = jnp.sum(jnp.where(blk == ki, sel_ref[h], 0.0), axis=1, keepdims=True)
        pen = jnp.where(ki == qi, causal_pen, jnp.where(picked > 0.5, 0.0, NEG))
        s = _dot_nt(qm_ref[h], kb) + bias_ref[h, 0] + pen
        _online_softmax_step(s, vb, m_ref, l_ref, acc_ref, h)

    @pl.when(ki == 0)
    def _():
        o_ref[0] = jnp.where(head == 0, acc_ref[0] / l_ref[0], acc_ref[1] / l_ref[1])


def _kmean_kernel(k_ref, o_ref):
    t = k_ref.shape[1]
    o_ref[0] = jnp.sum(k_ref[0].reshape(t // MOBA_BLOCK, MOBA_BLOCK, GROUP_WIDTH), axis=1) * (1.0 / MOBA_BLOCK)


def _moba_prompt(qag, kv, bias):
    b, t, _ = qag.shape
    nb = t // MOBA_BLOCK
    kmean = pl.pallas_call(
        _kmean_kernel, grid=(b,),
        in_specs=[pl.BlockSpec((1, t, GROUP_WIDTH), lambda i: (i, 0, 0))],
        out_specs=pl.BlockSpec((1, nb, GROUP_WIDTH), lambda i: (i, 0, 0)),
        out_shape=jax.ShapeDtypeStruct((b, nb, GROUP_WIDTH), F32),
        compiler_params=_cparams("parallel"),
    )(kv)
    qt, kt = _tri_tables(t // TQ)
    return pl.pallas_call(
        _moba_kernel,
        grid_spec=pltpu.PrefetchScalarGridSpec(
            num_scalar_prefetch=2, grid=(b, 2, qt.shape[0]),
            in_specs=_attn_specs(4, 0, 2) + [
                pl.BlockSpec((1, nb, LANES), lambda b, p, t, qt, kt: (b, 0, p)),
                pl.BlockSpec((2, 1, TQ, TQ), lambda b, p, t, qt, kt: (p, jnp.minimum(qt[t] - kt[t], 2), 0, 0))],
            out_specs=_attn_out_spec(),
            scratch_shapes=[pltpu.VMEM((2, TQ, LANES), BF16), pltpu.VMEM((2, TQ, nb), F32),
                            pltpu.VMEM((2, TQ, 1), F32), pltpu.VMEM((2, TQ, 1), F32),
                            pltpu.VMEM((2, TQ, LANES), F32)]),
        out_shape=jax.ShapeDtypeStruct((b, t, GROUP_WIDTH), F32),
        compiler_params=_cparams("parallel", "parallel", "arbitrary"),
    )(qt, kt, qag, kv, kv, kmean, bias)


def _diff_lambda(dl, lam_init):
    return (jnp.exp(jnp.sum(dl[0:1] * dl[1:2], axis=1, keepdims=True))
            - jnp.exp(jnp.sum(dl[2:3] * dl[3:4], axis=1, keepdims=True)) + lam_init)


def _diff_kernel(qt_ref, kt_ref, q_ref, k_ref, v_ref, bias_ref, dl_ref, sub_ref, o_ref,
                 qm_ref, m_ref, l_ref, acc_ref, *, lam_init):
    t = pl.program_id(2)
    qi, ki = qt_ref[t], kt_ref[t]
    lane = lax.broadcasted_iota(jnp.int32, (TQ, LANES), 1)
    head = lane // HEAD_DIM

    @pl.when(ki == qi)
    def _():
        q = q_ref[0]
        for j in range(4):
            qm_ref[j] = jnp.where(lane // DIFF_HALF == j, q, 0.0).astype(BF16)
        m_ref[...] = jnp.full_like(m_ref, NEG)
        l_ref[...] = jnp.zeros_like(l_ref)
        acc_ref[...] = jnp.zeros_like(acc_ref)

    kb = k_ref[0].astype(BF16)
    vb = v_ref[0].astype(BF16)
    r = lax.broadcasted_iota(jnp.int32, (TQ, TQ), 0)
    c = lax.broadcasted_iota(jnp.int32, (TQ, TQ), 1)
    pen = jnp.where(jnp.logical_or(ki < qi, c <= r), 0.0, NEG)
    for j in range(4):
        s = _dot_nt(qm_ref[j], kb) * DIFF_HALF ** -0.5 + bias_ref[j // 2, 0] + pen
        _online_softmax_step(s, vb, m_ref, l_ref, acc_ref, j)

    @pl.when(ki == 0)
    def _():
        lam = _diff_lambda(dl_ref[...], lam_init)
        o = jnp.where(head == 0,
                      acc_ref[0] / l_ref[0] - lam * (acc_ref[1] / l_ref[1]),
                      acc_ref[2] / l_ref[2] - lam * (acc_ref[3] / l_ref[3]))
        o2 = o * o
        ms = jnp.where(head == 0,
                       jnp.sum(jnp.where(head == 0, o2, 0.0), axis=1, keepdims=True),
                       jnp.sum(jnp.where(head == 1, o2, 0.0), axis=1, keepdims=True)) * (1.0 / HEAD_DIM)
        o_ref[0] = o * lax.rsqrt(ms + LN_EPS) * sub_ref[...] * (1.0 - lam_init)


def _diff_prompt(qag, kv, bias, dl, subln, lam_init):
    b, t, _ = qag.shape
    qt, kt = _tri_tables(t // TQ)
    return pl.pallas_call(
        functools.partial(_diff_kernel, lam_init=lam_init),
        grid_spec=pltpu.PrefetchScalarGridSpec(
            num_scalar_prefetch=2, grid=(b, 2, qt.shape[0]),
            in_specs=_attn_specs(8, 8, 10) + [
                pl.BlockSpec((2, 1, TQ, TQ), lambda b, p, t, qt, kt: (p, jnp.minimum(qt[t] - kt[t], 2), 0, 0)),
                pl.BlockSpec(dl.shape, lambda b, p, t, qt, kt: (0, 0)),
                pl.BlockSpec((1, LANES), lambda b, p, t, qt, kt: (0, 0))],
            out_specs=_attn_out_spec(),
            scratch_shapes=[pltpu.VMEM((4, TQ, LANES), BF16), pltpu.VMEM((4, TQ, 1), F32),
                            pltpu.VMEM((4, TQ, 1), F32), pltpu.VMEM((4, TQ, LANES), F32)]),
        out_shape=jax.ShapeDtypeStruct((b, t, GROUP_WIDTH), F32),
        compiler_params=_cparams("parallel", "parallel", "arbitrary"),
    )(qt, kt, qag, kv, kv, bias, dl, subln[:, :LANES])


def _prompt_bias_tiles(tab):
    i = np.arange(TQ)
    d = i[:, None] - i[None, :]
    bucket = _bucket_table(2 * TQ)
    idx = np.stack([bucket[np.maximum(d, 0)], bucket[d + TQ], np.full((TQ, TQ), bucket[-1])])
    assert (bucket[TQ + 1:] == bucket[-1]).all() and (_bucket_table(4 * TQ)[TQ:] == bucket[-1]).all()
    return jnp.transpose(tab[jnp.asarray(idx)], (3, 0, 1, 2))


def _expand_rows(x4):
    x = jnp.concatenate([x4] * N_HEADS, axis=0)
    rh = lax.broadcasted_iota(jnp.int32, x.shape, 0) // SAMPLE_ROWS
    lh = lax.broadcasted_iota(jnp.int32, x.shape, 1) // HEAD_DIM
    return jnp.where(rh == lh, x, 0.0)


def _collapse_rows(x):
    rh = lax.broadcasted_iota(jnp.int32, x.shape, 0) // SAMPLE_ROWS
    lh = lax.broadcasted_iota(jnp.int32, x.shape, 1) // HEAD_DIM
    x = jnp.where(rh == lh, x, 0.0)
    out = x[0:SAMPLE_ROWS]
    for h in range(1, N_HEADS):
        out = out + x[h * SAMPLE_ROWS:(h + 1) * SAMPLE_ROWS]
    return out


def _paged_kernel(pt_ref, q_ref, new_ref, *rest, g_pages, lam_init):
    page_refs = rest[:g_pages]
    bm_ref, bd_ref, tri_ref, dl_ref, sub_ref, o_ref = rest[g_pages:g_pages + 6]
    (qb_ref, qc_ref, qd_ref, qbf_ref, mb_m, mb_l, mb_g, mb_acc,
     run_ref, accc_ref, md_ref, ld_ref, accd_ref) = rest[g_pages + 6:]
    s_idx = pl.program_id(1)
    n_steps = pl.num_programs(1)
    nblk = mb_m.shape[0] - 1
    w = GROUP_WIDTH
    rows = N_HEADS * SAMPLE_ROWS
    page = new_ref.shape[1]
    qq = lax.broadcasted_iota(jnp.int32, (rows, page), 0) % SAMPLE_ROWS
    kk = lax.broadcasted_iota(jnp.int32, (rows, page), 1)
    tri = tri_ref[...]

    def slot(pg, i):
        return pg[:, i * w:(i + 1) * w]

    def moba_block(pages, biases, pens, n):
        ss, ksum = [], None
        for pg, bias, pen in zip(pages, biases, pens):
            k = slot(pg, 0)
            sc = _dot_nt(qb_ref[...], k.astype(BF16)) + bias
            ss.append(sc if pen is None else sc + pen)
            cs = jnp.sum(k, axis=0, keepdims=True)
            ksum = cs if ksum is None else ksum + cs
        m = ss[0].max(axis=1, keepdims=True)
        for sc in ss[1:]:
            m = jnp.maximum(m, sc.max(axis=1, keepdims=True))
        l, acc = None, None
        for pg, sc in zip(pages, ss):
            p = jnp.exp(sc - m)
            pl_ = jnp.sum(p, axis=1, keepdims=True)
            pa = _dot(p.astype(BF16), slot(pg, 1).astype(BF16))
            l = pl_ if l is None else l + pl_
            acc = pa if acc is None else acc + pa
        mb_m[n] = m
        mb_l[n] = l
        mb_acc[n] = acc
        mb_g[n] = jnp.sum(qbf_ref[...] * (ksum * (1.0 / MOBA_BLOCK)), axis=1, keepdims=True)

    def stick_page(pg, valid):
        z = _dot_nt(qc_ref[...], slot(pg, 2).astype(BF16))
        lk = -_softplus(z)
        if valid is not None:
            lk = jnp.where(valid, lk, 0.0)
        hi = lk.astype(BF16)
        lo = (lk - hi.astype(F32)).astype(BF16)
        cum = _dot(hi, tri) + _dot(lo, tri)
        e = jnp.exp(z + cum + run_ref[...])
        if valid is not None:
            e = jnp.where(valid, e, 0.0)
        accc_ref[...] += _dot(e.astype(BF16), slot(pg, 3).astype(BF16))
        run_ref[...] += cum[:, 0:1]

    def diff_pages(pages, biases, pens):
        ss = []
        for pg, bias, pen in zip(pages, biases, pens):
            sc = _dot_nt(qd_ref[...], slot(pg, 4).astype(BF16)) * DIFF_HALF ** -0.5 + bias
            ss.append(sc if pen is None else sc + pen)
        mx = ss[0]
        for sc in ss[1:]:
            mx = jnp.maximum(mx, sc)
        m_old = md_ref[...]
        m_new = jnp.maximum(m_old, mx.max(axis=1, keepdims=True))
        alpha = jnp.exp(m_old - m_new)
        psum, acc = None, None
        for pg, sc in zip(pages, ss):
            p = jnp.exp(sc - m_new)
            pa = _dot(p.astype(BF16), slot(pg, 5).astype(BF16))
            psum = p if psum is None else psum + p
            acc = pa if acc is None else acc + pa
        ld_ref[...] = alpha * ld_ref[...] + jnp.sum(psum, axis=1, keepdims=True)
        accd_ref[...] = alpha * accd_ref[...] + acc
        md_ref[...] = m_new

    @pl.when(s_idx == 0)
    def _():
        q = q_ref[0]
        qbx = _expand_rows(q[:, 0:w])
        qbf_ref[...] = qbx
        qb_ref[...] = (qbx * HEAD_DIM ** -0.5).astype(BF16)
        qc_ref[...] = (_expand_rows(q[:, w:2 * w]) * HEAD_DIM ** -0.5).astype(BF16)
        qdx = _expand_rows(q[:, 2 * w:3 * w])
        half = lax.broadcasted_iota(jnp.int32, qdx.shape, 1) // DIFF_HALF % 2
        qd_ref[0:rows] = jnp.where(half == 0, qdx, 0.0).astype(BF16)
        qd_ref[rows:2 * rows] = jnp.where(half == 1, qdx, 0.0).astype(BF16)
        run_ref[...] = jnp.zeros_like(run_ref)
        accc_ref[...] = jnp.zeros_like(accc_ref)
        md_ref[...] = jnp.full_like(md_ref, NEG)
        ld_ref[...] = jnp.zeros_like(ld_ref)
        accd_ref[...] = jnp.zeros_like(accd_ref)
        new = new_ref[0]
        causal_pen = jnp.where(kk <= qq, 0.0, NEG)
        moba_block([new], [bm_ref[0]], [causal_pen], nblk)
        stick_page(new, kk < qq)
        diff_pages([new], [bd_ref[0]], [jnp.concatenate([causal_pen, causal_pen], axis=0)])

    pages = [r[0, 0] for r in page_refs]
    bm0 = jnp.where(s_idx == 0, bm_ref[1], bm_ref[2])
    bd0 = jnp.where(s_idx == 0, bd_ref[1], bd_ref[2])
    bms = [bm0] + [bm_ref[2]] * (g_pages - 1)
    bds = [bd0] + [bd_ref[2]] * (g_pages - 1)
    per_block = MOBA_BLOCK // page
    for i in range(0, g_pages, per_block):
        n = nblk - 1 - (s_idx * g_pages + i) // per_block
        moba_block(pages[i:i + per_block], bms[i:i + per_block], [None] * per_block, n)
    for pg in pages:
        stick_page(pg, None)
    diff_pages(pages, bds, [None] * g_pages)

    @pl.when(s_idx == n_steps - 1)
    def _():
        blkf = lax.broadcasted_iota(jnp.int32, (nblk, rows, 1), 0).astype(F32)
        g = mb_g[0:nblk]
        sel = jnp.zeros((nblk, rows, 1), F32)
        for _ in range(min(MOBA_TOPK, nblk)):
            mx = jnp.max(g, axis=0, keepdims=True)
            idx = jnp.min(jnp.where(g == mx, blkf, float(nblk)), axis=0, keepdims=True)
            hit = blkf == idx
            sel = jnp.where(hit, 1.0, sel)
            g = jnp.where(hit, -jnp.inf, g)
        m_own = mb_m[nblk]
        m_tot = jnp.maximum(m_own, jnp.max(jnp.where(sel > 0.5, mb_m[0:nblk], NEG), axis=0))
        wgt = jnp.where(sel > 0.5, jnp.exp(mb_m[0:nblk] - m_tot), 0.0)
        w_own = jnp.exp(m_own - m_tot)
        l_tot = w_own * mb_l[nblk] + jnp.sum(wgt * mb_l[0:nblk], axis=0)
        a_tot = w_own * mb_acc[nblk] + jnp.sum(wgt * mb_acc[0:nblk], axis=0)
        o_ref[0, :, 0:w] = _collapse_rows(a_tot / l_tot)
        o_ref[0, :, w:2 * w] = _collapse_rows(accc_ref[...])
        lam = _diff_lambda(dl_ref[...], lam_init)
        od = (accd_ref[0:rows] / ld_ref[0:rows] - lam * (accd_ref[rows:2 * rows] / ld_ref[rows:2 * rows]))
        od = _collapse_rows(od)
        od2 = od * od
        lh = lax.broadcasted_iota(jnp.int32, od.shape, 1) // HEAD_DIM
        ms = jnp.zeros_like(od)
        for h in range(N_HEADS):
            ms = jnp.where(lh == h, jnp.sum(jnp.where(lh == h, od2, 0.0), axis=1, keepdims=True), ms)
        o_ref[0, :, 2 * w:3 * w] = od * lax.rsqrt(ms * (1.0 / HEAD_DIM) + LN_EPS) * sub_ref[...] * (1.0 - lam_init)


def _paged_attention(q3, new_page, cache, page_table, layer, bm, bd, dl, subln, lam_init):
    b = q3.shape[0]
    n_pages = page_table.shape[1]
    page = cache.shape[2]
    g_pages = min(PAGES_PER_STEP, n_pages)
    assert n_pages % g_pages == 0 and MOBA_BLOCK % page == 0 and g_pages % (MOBA_BLOCK // page) == 0
    nblk = n_pages * page // MOBA_BLOCK
    rows = N_HEADS * SAMPLE_ROWS
    w = GROUP_WIDTH

    def page_spec(i):
        return pl.BlockSpec((1, 1, page, 6 * w),
                            lambda bi, s, pt: (pt[bi, n_pages - 1 - (s * g_pages + i)], layer, 0, 0))

    const2 = lambda bi, s, pt: (0, 0)
    const3 = lambda bi, s, pt: (0, 0, 0)
    return pl.pallas_call(
        functools.partial(_paged_kernel, g_pages=g_pages, lam_init=lam_init),
        grid_spec=pltpu.PrefetchScalarGridSpec(
            num_scalar_prefetch=1, grid=(b, n_pages // g_pages),
            in_specs=[pl.BlockSpec((1, SAMPLE_ROWS, 3 * w), lambda bi, s, pt: (bi, 0, 0)),
                      pl.BlockSpec((1, page, 6 * w), lambda bi, s, pt: (bi, 0, 0))]
                     + [page_spec(i) for i in range(g_pages)]
                     + [pl.BlockSpec(bm.shape, const3), pl.BlockSpec(bd.shape, const3),
                        pl.BlockSpec((page, page), const2), pl.BlockSpec(dl.shape, const2),
                        pl.BlockSpec((1, w), const2)],
            out_specs=pl.BlockSpec((1, SAMPLE_ROWS, 3 * w), lambda bi, s, pt: (bi, 0, 0)),
            scratch_shapes=[pltpu.VMEM((rows, w), BF16), pltpu.VMEM((rows, w), BF16),
                            pltpu.VMEM((2 * rows, w), BF16), pltpu.VMEM((rows, w), F32),
                            pltpu.VMEM((nblk + 1, rows, 1), F32), pltpu.VMEM((nblk + 1, rows, 1), F32),
                            pltpu.VMEM((nblk + 1, rows, 1), F32), pltpu.VMEM((nblk + 1, rows, w), F32),
                            pltpu.VMEM((rows, 1), F32), pltpu.VMEM((rows, w), F32),
                            pltpu.VMEM((2 * rows, 1), F32), pltpu.VMEM((2 * rows, 1), F32),
                            pltpu.VMEM((2 * rows, w), F32)]),
        out_shape=jax.ShapeDtypeStruct((b, SAMPLE_ROWS, 3 * w), F32),
        compiler_params=_cparams("parallel", "arbitrary"),
    )(page_table, q3, new_page, *([cache] * g_pages), bm, bd, _suffix_tri(page), dl, subln)


def _sample_bias(tab, page, ts):
    bucket = _bucket_table(2 * page + SAMPLE_ROWS)
    assert (bucket[page + 1:] == bucket[-1]).all()
    q = np.minimum(np.arange(SAMPLE_ROWS), ts - 1)
    k = np.arange(page)
    idx = np.stack([bucket[np.maximum(q[:, None] - k[None, :], 0)],
                    bucket[page + q[:, None] - k[None, :]],
                    np.full((SAMPLE_ROWS, page), bucket[-1])])
    b = jnp.transpose(tab[jnp.asarray(idx)], (0, 3, 1, 2))
    return b.reshape(3, N_HEADS * SAMPLE_ROWS, page)


def _post_kernel(ya_ref, yb_ref, yc_ref, yd_ref, x_ref, wo_ref, ln1_ref, wq_ref, mem_ref, wxo_ref, ln2_ref,
                 o_ref, *, alpha):
    w = GROUP_WIDTH
    mixed = None
    for i, y_ref in enumerate((ya_ref, yb_ref, yc_ref, yd_ref)):
        part = _dot(y_ref[0].astype(BF16), wo_ref[i * w:(i + 1) * w, :])
        mixed = part if mixed is None else mixed + part
    x1 = _layer_norm(alpha * x_ref[0] + mixed, ln1_ref[0:1], ln1_ref[1:2])
    qx = _dot(x1.astype(BF16), wq_ref[...])
    mem = mem_ref[0]
    kb = mem[:, :w].astype(BF16)
    vb = mem[:, w:].astype(BF16)
    head = lax.broadcasted_iota(jnp.int32, qx.shape, 1) // HEAD_DIM
    o = jnp.zeros_like(qx)
    for h in range(N_HEADS):
        s = _dot_nt(jnp.where(head == h, qx, 0.0).astype(BF16), kb) * HEAD_DIM ** -0.5
        p = jnp.exp(s - jnp.max(s, axis=1, keepdims=True))
        p = p / jnp.sum(p, axis=1, keepdims=True)
        o = jnp.where(head == h, _dot(p.astype(BF16), vb), o)
    xo = _dot(o.astype(BF16), wxo_ref[...])
    o_ref[0] = _layer_norm(alpha * x1 + xo, ln2_ref[0:1], ln2_ref[1:2])


def _post(ys, x, w_out, ln1, w_xq, mem_kv, w_xo, ln2, alpha, tm):
    b, t, d = x.shape
    w = GROUP_WIDTH
    tm = min(tm, t)
    m_len = mem_kv.shape[1]
    row = lambda width: pl.BlockSpec((1, tm, width), lambda i, j: (i, j, 0))
    const = lambda shape: pl.BlockSpec(shape, lambda i, j: (0,) * len(shape))
    return pl.pallas_call(
        functools.partial(_post_kernel, alpha=alpha),
        grid=(b, t // tm),
        in_specs=[row(w)] * 4 + [row(d), const(w_out.shape), const(ln1.shape), const(w_xq.shape),
                                 pl.BlockSpec((1, m_len, 2 * w), lambda i, j: (i, 0, 0)),
                                 const(w_xo.shape), const(ln2.shape)],
        out_specs=row(d),
        out_shape=jax.ShapeDtypeStruct((b, t, d), F32),
        compiler_params=_cparams("parallel", "parallel"),
    )(*ys, x, w_out, ln1, w_xq, mem_kv, w_xo, ln2)


def _route(x, wr_ref, br_ref):
    logits = jnp.dot(x, wr_ref[...], preferred_element_type=F32, precision=lax.Precision.HIGHEST) + br_ref[...]
    lane = lax.broadcasted_iota(jnp.int32, logits.shape, 1)
    lanef = lane.astype(F32)
    is_grp = jnp.logical_and(lane >= N_EXPERTS, lane < N_EXPERTS + N_GROUPS)
    lg = jnp.where(is_grp, logits, -jnp.inf)
    gmax = jnp.max(lg, axis=1, keepdims=True)
    pg_sel = 1.0 / jnp.sum(jnp.exp(lg - gmax), axis=1, keepdims=True)
    gsel = jnp.min(jnp.where(lg == gmax, lanef, float(LANES)), axis=1, keepdims=True) - N_EXPERTS
    in_grp = jnp.logical_and(lane < N_EXPERTS, (lane // EXPERTS_PER_GROUP).astype(F32) == gsel)
    le = jnp.where(in_grp, logits, -jnp.inf)
    pe = jnp.exp(le - jnp.max(le, axis=1, keepdims=True))
    pe = pe / jnp.sum(pe, axis=1, keepdims=True)
    cand = jnp.where(in_grp, pe, -1.0)
    p1 = jnp.max(cand, axis=1, keepdims=True)
    i1 = jnp.min(jnp.where(cand == p1, lanef, float(LANES)), axis=1, keepdims=True)
    cand = jnp.where(lanef == i1, -1.0, cand)
    p2 = jnp.max(cand, axis=1, keepdims=True)
    i2 = jnp.min(jnp.where(cand == p2, lanef, float(LANES)), axis=1, keepdims=True)
    top = jnp.where(lanef == i1, p1, 0.0) + jnp.where(lanef == i2, p2, 0.0)
    return pg_sel * top / (p1 + p2)


def _moe_kernel(x_ref, wr_ref, br_ref, wg_ref, wu_ref, wd_ref, ln_ref, o_ref, xb_ref, cw_ref, acc_ref, *, alpha):
    e = pl.program_id(1)

    @pl.when(e == 0)
    def _():
        x = x_ref[...]
        xb_ref[...] = x.astype(BF16)
        cw_ref[...] = _route(x, wr_ref, br_ref)
        acc_ref[...] = jnp.zeros_like(acc_ref)

    xb = xb_ref[...]
    gate = _dot(xb, wg_ref[0])
    hid = gate * _sigmoid(gate) * _dot(xb, wu_ref[0])
    y = _dot(hid.astype(BF16), wd_ref[0])
    lane = lax.broadcasted_iota(jnp.int32, cw_ref.shape, 1)
    cw = jnp.sum(jnp.where(lane == e, cw_ref[...], 0.0), axis=1, keepdims=True)
    acc_ref[...] += cw * y

    @pl.when(e == pl.num_programs(1) - 1)
    def _():
        o_ref[...] = _layer_norm(alpha * x_ref[...] + acc_ref[...], ln_ref[0:1], ln_ref[1:2])


def _moe(x, w_router, b_router, w_gate, w_up, w_down, ln3, alpha, tm):
    n, d = x.shape
    tm = min(tm, n)
    n_e, _, d_ff = w_gate.shape
    return pl.pallas_call(
        functools.partial(_moe_kernel, alpha=alpha),
        grid=(n // tm, n_e),
        in_specs=[pl.BlockSpec((tm, d), lambda i, e: (i, 0)),
                  pl.BlockSpec(w_router.shape, lambda i, e: (0, 0)),
                  pl.BlockSpec(b_router.shape, lambda i, e: (0, 0)),
                  pl.BlockSpec((1, d, d_ff), lambda i, e: (e, 0, 0)),
                  pl.BlockSpec((1, d, d_ff), lambda i, e: (e, 0, 0)),
                  pl.BlockSpec((1, d_ff, d), lambda i, e: (e, 0, 0)),
                  pl.BlockSpec(ln3.shape, lambda i, e: (0, 0))],
        out_specs=pl.BlockSpec((tm, d), lambda i, e: (i, 0)),
        out_shape=jax.ShapeDtypeStruct((n, d), F32),
        scratch_shapes=[pltpu.VMEM((tm, d), BF16), pltpu.VMEM((tm, LANES), F32), pltpu.VMEM((tm, d), F32)],
        compiler_params=_cparams("parallel", "arbitrary"),
    )(x, w_router, b_router, w_gate, w_up, w_down, ln3)


def _block_diag(wh):
    h, d, _ = wh.shape
    eye = jnp.eye(h, dtype=wh.dtype)
    return (eye[:, None, :, None] * wh[:, :, None, :]).reshape(h * d, h * d)


def _layer_params(l, depth, p):
    w = GROUP_WIDTH
    w_in = p['w_in'][l]
    cols = lambda idx: jnp.concatenate([w_in[:, i * w:(i + 1) * w] for i in idx], axis=1).astype(BF16)
    d = w_in.shape[0]
    w_router = jnp.zeros((d, LANES), F32)
    w_router = w_router.at[:, :N_EXPERTS].set(p['w_re'][l]).at[:, N_EXPERTS:N_EXPERTS + N_GROUPS].set(p['w_rg'][l])
    b_router = jnp.zeros((1, LANES), F32)
    b_router = b_router.at[0, :N_EXPERTS].set(p['b_re'][l]).at[0, N_EXPERTS:N_EXPERTS + N_GROUPS].set(p['b_rg'][l])
    zeros = jnp.zeros((1, w), F32)
    return dict(
        w_q=cols((0, 1, 2, 5, 8)),
        w_kv=cols((3, 4, 6, 7, 9, 10)),
        pvec=jnp.concatenate([p['conv_b'][l][None], p['lru_ba'][l][None], p['lru_bx'][l][None],
                              p['lru_lambda'][l][None], p['conv_w'][l]], axis=0),
        w_gates=jnp.concatenate([_block_diag(p['lru_wa'][l]), _block_diag(p['lru_wx'][l])], axis=1).astype(BF16),
        dl=p['diff_lambda'][l],
        subln=jnp.tile(p['diff_subln'][l], N_HEADS)[None],
        lam_init=0.8 - 0.6 * math.exp(-0.3 * l),
        w_out=p['w_out'][l].astype(BF16),
        ln1=jnp.stack([p['ln1_g'][l], p['ln1_b'][l]]),
        w_xq=p['w_xq'][l].astype(BF16),
        w_xo=p['w_xo'][l].astype(BF16),
        ln2=jnp.stack([p['ln2_g'][l], p['ln2_b'][l]]),
        w_router=w_router, b_router=b_router,
        w_gate=p['w_gate'][l].astype(BF16), w_up=p['w_up'][l].astype(BF16), w_down=p['w_down'][l].astype(BF16),
        ln3=jnp.stack([p['ln3_g'][l], p['ln3_b'][l]]),
        w_xkv=p['w_xkv'][l].astype(BF16),
    )


def _pack_state(conv_buf, h0):
    b, _, w = conv_buf.shape
    pad = jnp.zeros((b, SUBLANES - CONV_WIDTH, w), F32)
    return jnp.concatenate([conv_buf, h0[:, None], pad], axis=1)


def _finish_layer(x, ys, mem_kv, lp, alpha, tm):
    b, t, d = x.shape
    x2 = _post(ys, x, lp['w_out'], lp['ln1'], lp['w_xq'], mem_kv, lp['w_xo'], lp['ln2'], alpha, tm)
    x3 = _moe(x2.reshape(b * t, d), lp['w_router'], lp['b_router'], lp['w_gate'], lp['w_up'], lp['w_down'],
              lp['ln3'], alpha, 512)
    return x3.reshape(b, t, d)


def kernel(x_prompt, x_sample, cache_kv, cache_mem_kv, state_rglru_h, state_conv, page_table, mem_prompt, rel_bias, w_in, conv_w, conv_b, lru_wa, lru_ba, lru_wx, lru_bx, lru_lambda, diff_lambda, diff_subln, w_out, ln1_g, ln1_b, w_xq, w_xkv, w_xo, ln2_g, ln2_b, w_rg, b_rg, w_re, b_re, w_gate, w_up, w_down, ln3_g, ln3_b):
    p = dict(w_in=w_in, conv_w=conv_w, conv_b=conv_b, lru_wa=lru_wa, lru_ba=lru_ba, lru_wx=lru_wx, lru_bx=lru_bx,
             lru_lambda=lru_lambda, diff_lambda=diff_lambda, diff_subln=diff_subln, w_out=w_out, ln1_g=ln1_g,
             ln1_b=ln1_b, w_xq=w_xq, w_xkv=w_xkv, w_xo=w_xo, ln2_g=ln2_g, ln2_b=ln2_b, w_rg=w_rg, b_rg=b_rg,
             w_re=w_re, b_re=b_re, w_gate=w_gate, w_up=w_up, w_down=w_down, ln3_g=ln3_g, ln3_b=ln3_b)
    depth = w_in.shape[0]
    alpha = (2 * depth) ** 0.25
    bp, tp, d = x_prompt.shape
    bs, ts, _ = x_sample.shape
    w = GROUP_WIDTH
    n_pool, _, page = cache_kv.shape[:3]
    assert tp % TQ == 0 and ts <= SAMPLE_ROWS and (page_table.shape[1] * page) % MOBA_BLOCK == 0
    m_len = mem_prompt.shape[1]
    cache = cache_kv.reshape(n_pool, depth, page, 6 * w)
    mem_sample = cache_mem_kv.reshape(bs, depth, m_len, 2 * w)

    tab_moba, tab_diff = rel_bias[:, :N_HEADS], rel_bias[:, N_HEADS:]
    pb_moba, pb_diff = _prompt_bias_tiles(tab_moba), _prompt_bias_tiles(tab_diff)
    sb_moba = _sample_bias(tab_moba, page, ts)
    sb_diff = jnp.concatenate([_sample_bias(tab_diff, page, ts)] * 2, axis=1)

    xp = x_prompt
    xs = jnp.pad(x_sample, ((0, 0), (0, SAMPLE_ROWS - ts), (0, 0)))
    st_p = jnp.zeros((bp, SUBLANES, w), F32)
    outs = dict(kv_p=[], mem_p=[], h_p=[], conv_p=[], kv_s=[], h_s=[], conv_s=[])
    for l in range(depth):
        lp = _layer_params(l, depth, p)
        mkv = _matmul(mem_prompt.reshape(bp * m_len, d), [lp['w_xkv']])[0].reshape(bp, m_len, 2 * w)

        qag, kv = _matmul(xp.reshape(bp * tp, d), [lp['w_q'], lp['w_kv']])
        qag, kv = qag.reshape(bp, tp, 5 * w), kv.reshape(bp, tp, 6 * w)
        y_a, st = _rglru(qag, st_p, lp['pvec'], lp['w_gates'], tc=TQ, tv=TQ)
        y_b = _moba_prompt(qag, kv, pb_moba)
        y_c = _stick_prompt(qag, kv)
        y_d = _diff_prompt(qag, kv, pb_diff, lp['dl'], lp['subln'], lp['lam_init'])
        xp = _finish_layer(xp, (y_a, y_b, y_c, y_d), mkv, lp, alpha, 512)
        outs['kv_p'].append(kv.reshape(bp, tp, 6, N_HEADS, HEAD_DIM))
        outs['mem_p'].append(mkv.reshape(bp, m_len, 2, N_HEADS, HEAD_DIM))
        outs['h_p'].append(st[:, CONV_WIDTH - 1])
        outs['conv_p'].append(st[:, :CONV_WIDTH - 1])

        qag, kv = _matmul(xs.reshape(bs * SAMPLE_ROWS, d), [lp['w_q'], lp['w_kv']])
        qag, kv = qag.reshape(bs, SAMPLE_ROWS, 5 * w), kv.reshape(bs, SAMPLE_ROWS, 6 * w)
        y_a, st = _rglru(qag, _pack_state(state_conv[:, l], state_rglru_h[:, l]), lp['pvec'], lp['w_gates'],
                         tc=SAMPLE_ROWS, tv=ts)
        new_page = jnp.pad(kv, ((0, 0), (0, page - SAMPLE_ROWS), (0, 0)))
        y3 = _paged_attention(qag[:, :, 2 * w:], new_page, cache, page_table, l, sb_moba, sb_diff,
                              lp['dl'], lp['subln'], lp['lam_init'])
        xs = _finish_layer(xs, (y_a, y3[:, :, :w], y3[:, :, w:2 * w], y3[:, :, 2 * w:]), mem_sample[:, l],
                           lp, alpha, SAMPLE_ROWS)
        outs['kv_s'].append(kv[:, :ts].reshape(bs, ts, 6, N_HEADS, HEAD_DIM))
        outs['h_s'].append(st[:, CONV_WIDTH - 1])
        outs['conv_s'].append(st[:, :CONV_WIDTH - 1])

    stack = lambda k: jnp.stack(outs[k], axis=1)
    return (xp, xs[:, :ts], stack('kv_p'), stack('mem_p'), stack('h_p'), stack('conv_p'),
            stack('kv_s'), stack('h_s'), stack('conv_s'))
```

```python
import functools
import math

import numpy as np
import jax
import jax.numpy as jnp
from jax import lax
from jax.experimental import pallas as pl
from jax.experimental.pallas import tpu as pltpu

F32 = jnp.float32
BF16 = jnp.bfloat16

HEAD_DIM = 64
N_HEADS = 4
GROUP_WIDTH = N_HEADS * HEAD_DIM
N_KV_SLOTS = 6
CONV_WIDTH = 4
LRU_C = 8.0
MOBA_BLOCK = 256
MOBA_TOPK = 3
DIFF_HALF = HEAD_DIM // 2
N_BUCKETS = 32
MAX_EXACT = N_BUCKETS // 2
MAX_DISTANCE = 128
N_GROUPS = 4
EXPERTS_PER_GROUP = 4
N_EXPERTS = N_GROUPS * EXPERTS_PER_GROUP
LN_EPS = 1e-5
NEG = -1e30
LOG2E = math.log2(math.e)
LANES = 128
SUBLANES = 8
VMEM_LIMIT = 56 * 1024 * 1024
TQ = MOBA_BLOCK
SAMPLE_ROWS = SUBLANES
PAGES_PER_STEP = 8


def _cparams(*sem):
    return pltpu.CompilerParams(dimension_semantics=sem, vmem_limit_bytes=VMEM_LIMIT)


def _softplus(x):
    return jnp.maximum(x, 0.0) + jnp.log1p(jnp.exp(-jnp.abs(x)))


def _sigmoid(x):
    return 1.0 / (1.0 + jnp.exp(-x))


def _dot(a, b, precision=None):
    return jnp.dot(a, b, preferred_element_type=F32, precision=precision)


def _dot_nt(a, b):
    return lax.dot_general(a, b, (((1,), (1,)), ((), ())), preferred_element_type=F32)


def _layer_norm(x, g, b):
    mu = jnp.mean(x, axis=-1, keepdims=True)
    xc = x - mu
    var = jnp.mean(xc * xc, axis=-1, keepdims=True)
    return xc * lax.rsqrt(var + LN_EPS) * g + b


def _bucket_table(n):
    d = np.arange(n)
    large = MAX_EXACT + (np.log(np.maximum(d, 1).astype(np.float32) / MAX_EXACT)
                         / math.log(MAX_DISTANCE / MAX_EXACT) * (N_BUCKETS - MAX_EXACT)).astype(np.int32)
    return np.where(d < MAX_EXACT, d, np.minimum(large, N_BUCKETS - 1)).astype(np.int32)


def _mm_kernel(*refs, n_out):
    x = refs[0][...].astype(BF16)
    for w_ref, o_ref in zip(refs[1:1 + n_out], refs[1 + n_out:]):
        o_ref[...] = _dot(x, w_ref[...])


def _matmul(x, ws, tm=512):
    m, k = x.shape
    tm = min(tm, m)
    return pl.pallas_call(
        functools.partial(_mm_kernel, n_out=len(ws)),
        grid=(m // tm,),
        in_specs=[pl.BlockSpec((tm, k), lambda i: (i, 0))] + [pl.BlockSpec(w.shape, lambda i: (0, 0)) for w in ws],
        out_specs=[pl.BlockSpec((tm, w.shape[1]), lambda i: (i, 0)) for w in ws],
        out_shape=[jax.ShapeDtypeStruct((m, w.shape[1]), F32) for w in ws],
        compiler_params=_cparams("parallel"),
    )(x, *ws)


def _proj_kernel(x_ref, wq_ref, wkvt_ref, q_ref, kvt_ref, kvb_ref):
    x = x_ref[0].astype(BF16)
    q_ref[0] = _dot(x, wq_ref[...])
    kvt = _dot_nt(wkvt_ref[...], x)
    kvt_ref[0] = kvt
    kvb_ref[0] = kvt.astype(BF16)


def _project(x, w_q, w_kv_t, tm=512):
    b, t, d = x.shape
    tm = min(tm, t)
    nq, nkv = w_q.shape[1], w_kv_t.shape[0]
    return pl.pallas_call(
        _proj_kernel,
        grid=(b, t // tm),
        in_specs=[pl.BlockSpec((1, tm, d), lambda i, j: (i, j, 0)),
                  pl.BlockSpec(w_q.shape, lambda i, j: (0, 0)),
                  pl.BlockSpec(w_kv_t.shape, lambda i, j: (0, 0))],
        out_specs=[pl.BlockSpec((1, tm, nq), lambda i, j: (i, j, 0)),
                   pl.BlockSpec((1, nkv, tm), lambda i, j: (i, 0, j)),
                   pl.BlockSpec((1, nkv, tm), lambda i, j: (i, 0, j))],
        out_shape=[jax.ShapeDtypeStruct((b, t, nq), F32), jax.ShapeDtypeStruct((b, nkv, t), F32),
                   jax.ShapeDtypeStruct((b, nkv, t), BF16)],
        compiler_params=_cparams("parallel", "parallel"),
    )(x, w_q, w_kv_t)


def _mem_kernel(m_ref, w_ref, o_ref):
    o_ref[0, 0] = _dot_nt(w_ref[...], m_ref[0].astype(BF16))


def _memory_kv_t(mem, w_xkv_t):
    b, m, d = mem.shape
    n = w_xkv_t.shape[0]
    return pl.pallas_call(
        _mem_kernel, grid=(b,),
        in_specs=[pl.BlockSpec((1, m, d), lambda i: (i, 0, 0)), pl.BlockSpec(w_xkv_t.shape, lambda i: (0, 0))],
        out_specs=pl.BlockSpec((1, 1, n, m), lambda i: (i, 0, 0, 0)),
        out_shape=jax.ShapeDtypeStruct((b, 1, n, m), F32),
        compiler_params=_cparams("parallel"),
    )(mem, w_xkv_t)


def _rglru_kernel(x_ref, g_ref, st_ref, pv_ref, w_ref, y_ref, so_ref, carry_ref, *, tc, tv):
    c = pl.program_id(1)

    @pl.when(c == 0)
    def _():
        carry_ref[...] = st_ref[0]

    w = GROUP_WIDTH
    x = x_ref[0]
    row = lax.broadcasted_iota(jnp.int32, (tc, w), 0)
    pv = pv_ref[...]
    conv_b, ba, bx, lam = pv[0:1], pv[1:2], pv[2:3], pv[3:4]
    prev = carry_ref[...]
    u = conv_b + pv[4 + CONV_WIDTH - 1:4 + CONV_WIDTH] * x
    for sft in range(1, CONV_WIDTH):
        xs = pltpu.roll(x, sft, 0)
        for i in range(sft):
            xs = jnp.where(row == i, prev[CONV_WIDTH - 1 + i - sft:CONV_WIDTH + i - sft], xs)
        u = u + pv[4 + CONV_WIDTH - 1 - sft:4 + CONV_WIDTH - sft] * xs
    gates = _dot(u.astype(BF16), w_ref[...])
    r = _sigmoid(gates[:, :w] + ba)
    ig = _sigmoid(gates[:, w:] + bx)
    log_a = -LRU_C * r * _softplus(-lam)
    a = jnp.exp(log_a)
    b = jnp.sqrt(-jnp.tanh(log_a) * (jnp.exp(2.0 * log_a) + 1.0)) * ig * u
    s = 1
    while s < tc:
        a_s = pltpu.roll(a, s, 0)
        b_s = pltpu.roll(b, s, 0)
        m = row >= s
        b = jnp.where(m, a * b_s + b, b)
        a = jnp.where(m, a * a_s, a)
        s *= 2
    h = b + a * prev[CONV_WIDTH - 1:CONV_WIDTH]
    gt = g_ref[0]
    gelu = 0.5 * gt * (1.0 + jnp.tanh(math.sqrt(2.0 / math.pi) * (gt + 0.044715 * (gt * gt * gt))))
    y_ref[0] = h * gelu
    carry_ref[0:CONV_WIDTH - 1] = x[tv - (CONV_WIDTH - 1):tv]
    carry_ref[CONV_WIDTH - 1:CONV_WIDTH] = h[tv - 1:tv]

    @pl.when(c == pl.num_programs(1) - 1)
    def _():
        so_ref[0] = carry_ref[...]


def _rglru(qag, state, pvec, w_gates, tc, tv):
    b, t, _ = qag.shape
    w = GROUP_WIDTH
    assert t % tc == 0 and tv >= CONV_WIDTH - 1
    return pl.pallas_call(
        functools.partial(_rglru_kernel, tc=tc, tv=tv),
        grid=(b, t // tc),
        in_specs=[pl.BlockSpec((1, tc, w), lambda i, c: (i, c, 0)),
                  pl.BlockSpec((1, tc, w), lambda i, c: (i, c, 1)),
                  pl.BlockSpec((1, SUBLANES, w), lambda i, c: (i, 0, 0)),
                  pl.BlockSpec((SUBLANES, w), lambda i, c: (0, 0)),
                  pl.BlockSpec((w, 2 * w), lambda i, c: (0, 0))],
        out_specs=[pl.BlockSpec((1, tc, w), lambda i, c: (i, c, 0)),
                   pl.BlockSpec((1, SUBLANES, w), lambda i, c: (i, 0, 0))],
        out_shape=[jax.ShapeDtypeStruct((b, t, w), F32), jax.ShapeDtypeStruct((b, SUBLANES, w), F32)],
        scratch_shapes=[pltpu.VMEM((SUBLANES, w), F32)],
        compiler_params=_cparams("parallel", "arbitrary"),
    )(qag, qag, state, pvec, w_gates)


def _tri_tables(nq):
    qs, ks = [], []
    for qi in range(nq):
        for ki in range(qi, -1, -1):
            qs.append(qi)
            ks.append(ki)
    return jnp.asarray(np.array(qs, np.int32)), jnp.asarray(np.array(ks, np.int32))


def _wide(x):
    return jnp.concatenate([x, x], axis=1)


def _values_with(vt, h, fill):
    rowh = lax.broadcasted_iota(jnp.int32, vt.shape, 0) // HEAD_DIM
    return jnp.where(rowh == h, vt, jnp.full((), fill, vt.dtype))


def _softmax_update(j, t2, m_cand, vt_aug, m_ref, acc_ref, shift=None):
    m_old = m_ref[j]
    m_new = jnp.maximum(m_old, m_cand)
    off = m_new if shift is None else m_new - shift
    p = jnp.exp2(t2 - _wide(off))
    acc_ref[j] = jnp.exp2(m_old - m_new) * acc_ref[j] + _dot_nt(p.astype(BF16), vt_aug)
    m_ref[j] = m_new


def _normalised(acc):
    return acc / pltpu.roll(acc, HEAD_DIM, 1)


def _moba_select(q, km, qi):
    blk = lax.broadcasted_iota(jnp.int32, (TQ, LANES), 1)
    blkf = blk.astype(F32)
    gate = _dot(q, km, precision=lax.Precision.HIGHEST)
    g = jnp.where(blk < qi, gate, -jnp.inf)
    sel = jnp.zeros((TQ, LANES), F32)
    for _ in range(MOBA_TOPK):
        mx = jnp.max(g, axis=1, keepdims=True)
        idx = jnp.min(jnp.where(g == mx, blkf, float(LANES)), axis=1, keepdims=True)
        hit = blkf == idx
        sel = jnp.where(hit, 1.0, sel)
        g = jnp.where(hit, -jnp.inf, g)
    return jnp.where(blk < qi, sel, 0.0)


def _diff_lambda(dl, lam_init):
    return (jnp.exp(jnp.sum(dl[0:1] * dl[1:2], axis=1, keepdims=True))
            - jnp.exp(jnp.sum(dl[2:3] * dl[3:4], axis=1, keepdims=True)) + lam_init)


def _mix_kernel(qt_ref, kt_ref, qb_ref, qc_ref, qd_ref, kv_ref,
                ntri_ref, km_ref, bias_b_ref, far_b_ref, bias_d_ref, far_d_ref, dl_ref, sub_ref,
                ob_ref, oc_ref, od_ref,
                qmb_ref, sel_ref, mb_ref, accb_ref, qmc_ref, run_ref, accc_ref, qmd_ref, md_ref, accd_ref,
                *, lam_init):
    t = pl.program_id(2)
    qi, ki = qt_ref[t], kt_ref[t]

    @pl.when(ki == qi)
    def _():
        qb = qb_ref[0]
        qb2 = qb * (HEAD_DIM ** -0.5 * LOG2E)
        qc2 = qc_ref[0] * (HEAD_DIM ** -0.5 * LOG2E)
        qd2 = qd_ref[0] * (DIFF_HALF ** -0.5 * LOG2E)
        for h in range(2):
            hs = slice(h * HEAD_DIM, (h + 1) * HEAD_DIM)
            sel_ref[:, h * LANES:(h + 1) * LANES] = _moba_select(qb[:, hs], km_ref[0, hs, :], qi).astype(BF16)
            qmb_ref[h] = qb2[:, hs].astype(BF16)
            qmc_ref[h] = qc2[:, hs].astype(BF16)
        for j in range(4):
            qmd_ref[j] = qd2[:, j * DIFF_HALF:(j + 1) * DIFF_HALF].astype(BF16)
        mb_ref[...] = jnp.full_like(mb_ref, NEG)
        md_ref[...] = jnp.full_like(md_ref, NEG)
        accb_ref[...] = jnp.zeros_like(accb_ref)
        accc_ref[...] = jnp.zeros_like(accc_ref)
        accd_ref[...] = jnp.zeros_like(accd_ref)
        run_ref[...] = jnp.zeros_like(run_ref)

    def moba_step(near, diag):
        kt = kv_ref[0, 0, 0]
        vt = kv_ref[0, 1, 0]
        if not diag:
            hit = lax.broadcasted_iota(jnp.int32, (LANES, LANES), 0) == ki
            e = jnp.where(hit, 1.0, 0.0).astype(BF16)
            z = jnp.zeros_like(e)
            onehot = jnp.concatenate([jnp.concatenate([e, z], axis=1), jnp.concatenate([z, e], axis=1)], axis=0)
            picked = _dot(sel_ref[...], onehot)
        for h in range(2):
            raw = _dot(qmb_ref[h], kt[h * HEAD_DIM:(h + 1) * HEAD_DIM])
            vt_aug = _values_with(vt, h, 1.0)
            row_pen = None if diag else jnp.where(picked[:, h * LANES:(h + 1) * LANES] > 0.5, 0.0, NEG)
            if near:
                t2 = raw + bias_b_ref[h, 0]
                m_cand = jnp.max(t2, axis=1, keepdims=True)
                m_cand = m_cand if diag else m_cand + row_pen
                _softmax_update(h, t2, m_cand, vt_aug, mb_ref, accb_ref, shift=row_pen)
            else:
                shift = far_b_ref[h, 0:1] + row_pen
                m_cand = jnp.max(raw, axis=1, keepdims=True) + shift
                _softmax_update(h, raw, m_cand, vt_aug, mb_ref, accb_ref, shift=shift)

    def stick_step(diag):
        kt = kv_ref[0, 2, 0]
        vt = kv_ref[0, 3, 0]
        ntri = ntri_ref[...]
        if diag:
            r = lax.broadcasted_iota(jnp.int32, (TQ, TQ), 0)
            c = lax.broadcasted_iota(jnp.int32, (TQ, TQ), 1)
            valid = c < r
        z2s, parts = [], []
        for h in range(2):
            z2 = _dot(qmc_ref[h], kt[h * HEAD_DIM:(h + 1) * HEAD_DIM])
            sp2 = jnp.maximum(z2, 0.0) + jnp.log2(1.0 + jnp.exp2(-jnp.abs(z2)))
            if diag:
                sp2 = jnp.where(valid, sp2, 0.0)
            hi = sp2.astype(BF16)
            z2s.append(z2)
            parts += [hi, (sp2 - hi.astype(F32)).astype(BF16)]
            run_new = run_ref[h] - jnp.broadcast_to(jnp.sum(sp2, axis=1, keepdims=True), (TQ, LANES))
            z2s.append(run_new)
        cums = _dot(jnp.concatenate(parts, axis=0), ntri)
        acc = accc_ref[...]
        for h in range(2):
            z2, run_new = z2s[2 * h], z2s[2 * h + 1]
            cum2 = cums[2 * h * TQ:(2 * h + 1) * TQ] + cums[(2 * h + 1) * TQ:(2 * h + 2) * TQ]
            e = jnp.exp2(z2 + cum2 + _wide(run_ref[h]))
            if diag:
                e = jnp.where(valid, e, 0.0)
            acc = acc + _dot_nt(e.astype(BF16), _values_with(vt, h, 0.0))
            run_ref[h] = run_new
        accc_ref[...] = acc

    def diff_step(near):
        kt = kv_ref[0, 4, 0]
        vt = kv_ref[0, 5, 0]
        vt_aug = [_values_with(vt, h, 1.0) for h in range(2)]
        for j in range(4):
            h = j // 2
            raw = _dot(qmd_ref[j], kt[j * DIFF_HALF:(j + 1) * DIFF_HALF])
            if near:
                t2 = raw + bias_d_ref[h, 0]
                _softmax_update(j, t2, jnp.max(t2, axis=1, keepdims=True), vt_aug[h], md_ref, accd_ref)
            else:
                shift = far_d_ref[h, 0:1]
                m_cand = jnp.max(raw, axis=1, keepdims=True) + shift
                _softmax_update(j, raw, m_cand, vt_aug[h], md_ref, accd_ref, shift=shift)

    @pl.when(ki == qi)
    def _():
        moba_step(True, True)
        stick_step(True)
        diff_step(True)

    @pl.when(qi - ki == 1)
    def _():
        moba_step(True, False)
        stick_step(False)
        diff_step(True)

    @pl.when(qi - ki >= 2)
    def _():
        moba_step(False, False)
        stick_step(False)
        diff_step(False)

    @pl.when(ki == 0)
    def _():
        head = lax.broadcasted_iota(jnp.int32, (TQ, LANES), 1) // HEAD_DIM
        ob_ref[0] = jnp.where(head == 0, _normalised(accb_ref[0]), _normalised(accb_ref[1]))
        oc_ref[0] = accc_ref[...]
        lam = _diff_lambda(dl_ref[...], lam_init)
        o = jnp.where(head == 0,
                      _normalised(accd_ref[0]) - lam * _normalised(accd_ref[1]),
                      _normalised(accd_ref[2]) - lam * _normalised(accd_ref[3]))
        o2 = o * o
        ms = jnp.where(head == 0,
                       jnp.sum(jnp.where(head == 0, o2, 0.0), axis=1, keepdims=True),
                       jnp.sum(jnp.where(head == 1, o2, 0.0), axis=1, keepdims=True)) * (1.0 / HEAD_DIM)
        od_ref[0] = o * lax.rsqrt(ms + LN_EPS) * sub_ref[...] * (1.0 - lam_init)


def _kmean_kernel(k_ref, ones_ref, o_ref):
    o_ref[0] = _dot(k_ref[0], ones_ref[...], precision=lax.Precision.HIGHEST)


def _suffix_tri(n, sign=1.0):
    i = np.arange(n)
    return jnp.asarray(sign * (i[:, None] >= i[None, :]).astype(np.float32), dtype=BF16)


def _prompt_mixers(qag, kvt, kvb, bias_b, far_b, bias_d, far_d, dl, subln, lam_init):
    b, t, _ = qag.shape
    kv5 = kvb.reshape(b, N_KV_SLOTS, 2, LANES, t)
    nb = t // MOBA_BLOCK
    assert nb <= LANES
    blk_mean = np.zeros((t, LANES), np.float32)
    blk_mean[np.arange(t), np.arange(t) // MOBA_BLOCK] = 1.0 / MOBA_BLOCK
    kmean_t = pl.pallas_call(
        _kmean_kernel, grid=(b,),
        in_specs=[pl.BlockSpec((1, GROUP_WIDTH, t), lambda i: (i, 0, 0)),
                  pl.BlockSpec((t, LANES), lambda i: (0, 0))],
        out_specs=pl.BlockSpec((1, GROUP_WIDTH, LANES), lambda i: (i, 0, 0)),
        out_shape=jax.ShapeDtypeStruct((b, GROUP_WIDTH, LANES), F32),
        compiler_params=_cparams("parallel"),
    )(kvt, jnp.asarray(blk_mean))
    qt, kt = _tri_tables(t // TQ)
    q_spec = lambda col: pl.BlockSpec((1, TQ, LANES), lambda b, p, t, qt, kt: (b, qt[t], col + p))
    kv_spec = pl.BlockSpec((1, N_KV_SLOTS, 1, LANES, TQ), lambda b, p, t, qt, kt: (b, 0, p, 0, kt[t]))
    bias_spec = pl.BlockSpec((2, 1, TQ, TQ), lambda b, p, t, qt, kt: (p, jnp.minimum(qt[t] - kt[t], 1), 0, 0))
    far_spec = pl.BlockSpec((2, SUBLANES, LANES), lambda b, p, t, qt, kt: (p, 0, 0))
    const2 = lambda shape: pl.BlockSpec(shape, lambda b, p, t, qt, kt: (0, 0))
    out_spec = pl.BlockSpec((1, TQ, LANES), lambda b, p, t, qt, kt: (b, qt[t], p))
    out = jax.ShapeDtypeStruct((b, t, GROUP_WIDTH), F32)
    stats = lambda n: pltpu.VMEM((n, TQ, LANES), F32)
    return pl.pallas_call(
        functools.partial(_mix_kernel, lam_init=lam_init),
        grid_spec=pltpu.PrefetchScalarGridSpec(
            num_scalar_prefetch=2, grid=(b, 2, qt.shape[0]),
            in_specs=[q_spec(4), q_spec(6), q_spec(8), kv_spec]
                     + [const2((TQ, TQ)), pl.BlockSpec((1, LANES, LANES), lambda b, p, t, qt, kt: (b, p, 0)),
                        bias_spec, far_spec, bias_spec, far_spec, const2(dl.shape), const2((1, LANES))],
            out_specs=[out_spec] * 3,
            scratch_shapes=[pltpu.VMEM((2, TQ, HEAD_DIM), BF16), pltpu.VMEM((TQ, 2 * LANES), BF16),
                            stats(2), stats(2),
                            pltpu.VMEM((2, TQ, HEAD_DIM), BF16), stats(2), pltpu.VMEM((TQ, LANES), F32),
                            pltpu.VMEM((4, TQ, DIFF_HALF), BF16), stats(4), stats(4)]),
        out_shape=[out] * 3,
        compiler_params=_cparams("parallel", "parallel", "arbitrary"),
    )(qt, kt, qag, qag, qag, kv5, _suffix_tri(TQ, -1.0), kmean_t,
      bias_b, far_b, bias_d, far_d, dl, subln[:, :LANES])


def _toeplitz(g):
    h, l = g.shape
    n = l // 2
    w = jnp.concatenate([g[:, :1], jnp.flip(g[:, 1:], axis=1)], axis=1)
    rep = jnp.tile(w, (1, n))[:, :n * (l - 1)].reshape(h, n, l - 1)
    return rep[:, :, :n]


def _prompt_bias(tab):
    n = TQ
    bucket = _bucket_table(4 * n)
    assert (bucket[n + 1:] == bucket[-1]).all()
    x = np.arange(2 * n)
    d = np.where(x < n, x, x - 2 * n)
    fvec = (tab * LOG2E).T
    g0 = jnp.where(jnp.asarray(d >= 0), fvec[:, bucket[np.maximum(d, 0)]], NEG)
    g1 = fvec[:, bucket[n + d]]
    tiles = jnp.stack([_toeplitz(g0), _toeplitz(g1)], axis=1)
    far = jnp.broadcast_to(fvec[:, bucket[-1]][:, None, None], (tab.shape[1], SUBLANES, LANES))
    return tiles, far


def _expand_rows(x4):
    x = jnp.concatenate([x4] * N_HEADS, axis=0)
    rh = lax.broadcasted_iota(jnp.int32, x.shape, 0) // SAMPLE_ROWS
    lh = lax.broadcasted_iota(jnp.int32, x.shape, 1) // HEAD_DIM
    return jnp.where(rh == lh, x, 0.0)


def _collapse_rows(x):
    rh = lax.broadcasted_iota(jnp.int32, x.shape, 0) // SAMPLE_ROWS
    lh = lax.broadcasted_iota(jnp.int32, x.shape, 1) // HEAD_DIM
    x = jnp.where(rh == lh, x, 0.0)
    out = x[0:SAMPLE_ROWS]
    for h in range(1, N_HEADS):
        out = out + x[h * SAMPLE_ROWS:(h + 1) * SAMPLE_ROWS]
    return out


def _paged_kernel(pt_ref, q_ref, new_ref, *rest, g_pages, lam_init):
    page_refs = rest[:g_pages]
    bm_ref, bd_ref, tri_ref, dl_ref, sub_ref, o_ref = rest[g_pages:g_pages + 6]
    (qb_ref, qc_ref, qd_ref, mb_m, mb_l, mb_g, mb_acc,
     run_ref, accc_ref, md_ref, ld_ref, accd_ref) = rest[g_pages + 6:]
    s_idx = pl.program_id(1)
    n_steps = pl.num_programs(1)
    nblk = mb_m.shape[0] - 1
    w = GROUP_WIDTH
    rows = N_HEADS * SAMPLE_ROWS
    page = new_ref.shape[2]
    qq = lax.broadcasted_iota(jnp.int32, (rows, page), 0) % SAMPLE_ROWS
    kk = lax.broadcasted_iota(jnp.int32, (rows, page), 1)
    tri = tri_ref[...]

    def slot(pg, i):
        return pg[i * w:(i + 1) * w, :].astype(BF16)

    def moba_block(pages, biases, pens, n):
        ss, gsum = [], None
        for pg, bias, pen in zip(pages, biases, pens):
            raw = _dot(qb_ref[...], slot(pg, 0))
            sc = raw + bias
            ss.append(sc if pen is None else sc + pen)
            gsum = raw if gsum is None else gsum + raw
        m = ss[0].max(axis=1, keepdims=True)
        for sc in ss[1:]:
            m = jnp.maximum(m, sc.max(axis=1, keepdims=True))
        psum, acc = None, None
        for pg, sc in zip(pages, ss):
            p = jnp.exp(sc - m)
            pa = _dot_nt(p.astype(BF16), slot(pg, 1))
            psum = p if psum is None else psum + p
            acc = pa if acc is None else acc + pa
        mb_m[n] = m
        mb_l[n] = jnp.sum(psum, axis=1, keepdims=True)
        mb_acc[n] = acc
        mb_g[n] = jnp.sum(gsum, axis=1, keepdims=True)

    def stick_page(pg, valid):
        z = _dot(qc_ref[...], slot(pg, 2))
        lk = -_softplus(z)
        if valid is not None:
            lk = jnp.where(valid, lk, 0.0)
        hi = lk.astype(BF16)
        lo = (lk - hi.astype(F32)).astype(BF16)
        cum = _dot(hi, tri) + _dot(lo, tri)
        e = jnp.exp(z + cum + run_ref[...])
        if valid is not None:
            e = jnp.where(valid, e, 0.0)
        accc_ref[...] += _dot_nt(e.astype(BF16), slot(pg, 3))
        run_ref[...] += jnp.sum(lk, axis=1, keepdims=True)

    def diff_pages(pages, biases, pens):
        ss = []
        for pg, bias, pen in zip(pages, biases, pens):
            sc = _dot(qd_ref[...], slot(pg, 4)) * DIFF_HALF ** -0.5 + bias
            ss.append(sc if pen is None else sc + pen)
        mx = ss[0]
        for sc in ss[1:]:
            mx = jnp.maximum(mx, sc)
        m_old = md_ref[...]
        m_new = jnp.maximum(m_old, mx.max(axis=1, keepdims=True))
        alpha = jnp.exp(m_old - m_new)
        psum, acc = None, None
        for pg, sc in zip(pages, ss):
            p = jnp.exp(sc - m_new)
            pa = _dot_nt(p.astype(BF16), slot(pg, 5))
            psum = p if psum is None else psum + p
            acc = pa if acc is None else acc + pa
        ld_ref[...] = alpha * ld_ref[...] + jnp.sum(psum, axis=1, keepdims=True)
        accd_ref[...] = alpha * accd_ref[...] + acc
        md_ref[...] = m_new

    @pl.when(s_idx == 0)
    def _():
        q = q_ref[0]
        qb_ref[...] = (_expand_rows(q[:, 0:w]) * HEAD_DIM ** -0.5).astype(BF16)
        qc_ref[...] = (_expand_rows(q[:, w:2 * w]) * HEAD_DIM ** -0.5).astype(BF16)
        qdx = _expand_rows(q[:, 2 * w:3 * w])
        half = lax.broadcasted_iota(jnp.int32, qdx.shape, 1) // DIFF_HALF % 2
        qd_ref[0:rows] = jnp.where(half == 0, qdx, 0.0).astype(BF16)
        qd_ref[rows:2 * rows] = jnp.where(half == 1, qdx, 0.0).astype(BF16)
        run_ref[...] = jnp.zeros_like(run_ref)
        accc_ref[...] = jnp.zeros_like(accc_ref)
        md_ref[...] = jnp.full_like(md_ref, NEG)
        ld_ref[...] = jnp.zeros_like(ld_ref)
        accd_ref[...] = jnp.zeros_like(accd_ref)
        new = new_ref[0]
        causal_pen = jnp.where(kk <= qq, 0.0, NEG)
        moba_block([new], [bm_ref[0]], [causal_pen], nblk)
        stick_page(new, kk < qq)
        diff_pages([new], [bd_ref[0]], [jnp.concatenate([causal_pen, causal_pen], axis=0)])

    pages = [r[0, 0] for r in page_refs]
    bm0 = jnp.where(s_idx == 0, bm_ref[1], bm_ref[2])
    bd0 = jnp.where(s_idx == 0, bd_ref[1], bd_ref[2])
    bms = [bm0] + [bm_ref[2]] * (g_pages - 1)
    bds = [bd0] + [bd_ref[2]] * (g_pages - 1)
    per_block = MOBA_BLOCK // page
    for i in range(0, g_pages, per_block):
        n = nblk - 1 - (s_idx * g_pages + i) // per_block
        moba_block(pages[i:i + per_block], bms[i:i + per_block], [None] * per_block, n)
    for pg in pages:
        stick_page(pg, None)
    diff_pages(pages, bds, [None] * g_pages)

    @pl.when(s_idx == n_steps - 1)
    def _():
        blkf = lax.broadcasted_iota(jnp.int32, (nblk, rows, 1), 0).astype(F32)
        g = mb_g[0:nblk]
        sel = jnp.zeros((nblk, rows, 1), F32)
        for _ in range(min(MOBA_TOPK, nblk)):
            mx = jnp.max(g, axis=0, keepdims=True)
            idx = jnp.min(jnp.where(g == mx, blkf, float(nblk)), axis=0, keepdims=True)
            hit = blkf == idx
            sel = jnp.where(hit, 1.0, sel)
            g = jnp.where(hit, -jnp.inf, g)
        m_own = mb_m[nblk]
        m_tot = jnp.maximum(m_own, jnp.max(jnp.where(sel > 0.5, mb_m[0:nblk], NEG), axis=0))
        wgt = jnp.where(sel > 0.5, jnp.exp(mb_m[0:nblk] - m_tot), 0.0)
        w_own = jnp.exp(m_own - m_tot)
        l_tot = w_own * mb_l[nblk] + jnp.sum(wgt * mb_l[0:nblk], axis=0)
        a_tot = w_own * mb_acc[nblk] + jnp.sum(wgt * mb_acc[0:nblk], axis=0)
        o_ref[0, :, 0:w] = _collapse_rows(a_tot / l_tot)
        o_ref[0, :, w:2 * w] = _collapse_rows(accc_ref[...])
        lam = _diff_lambda(dl_ref[...], lam_init)
        od = (accd_ref[0:rows] / ld_ref[0:rows] - lam * (accd_ref[rows:2 * rows] / ld_ref[rows:2 * rows]))
        od = _collapse_rows(od)
        od2 = od * od
        lh = lax.broadcasted_iota(jnp.int32, od.shape, 1) // HEAD_DIM
        ms = jnp.zeros_like(od)
        for h in range(N_HEADS):
            ms = jnp.where(lh == h, jnp.sum(jnp.where(lh == h, od2, 0.0), axis=1, keepdims=True), ms)
        o_ref[0, :, 2 * w:3 * w] = od * lax.rsqrt(ms * (1.0 / HEAD_DIM) + LN_EPS) * sub_ref[...] * (1.0 - lam_init)


def _paged_attention(q3, new_page, cache_t, page_table, layer, bm, bd, dl, subln, lam_init):
    b = q3.shape[0]
    n_pages = page_table.shape[1]
    page = cache_t.shape[3]
    g_pages = min(PAGES_PER_STEP, n_pages)
    assert n_pages % g_pages == 0 and MOBA_BLOCK % page == 0 and g_pages % (MOBA_BLOCK // page) == 0
    nblk = n_pages * page // MOBA_BLOCK
    rows = N_HEADS * SAMPLE_ROWS
    w = GROUP_WIDTH

    def page_spec(i):
        return pl.BlockSpec((1, 1, N_KV_SLOTS * w, page),
                            lambda bi, s, pt: (pt[bi, n_pages - 1 - (s * g_pages + i)], layer, 0, 0))

    const2 = lambda bi, s, pt: (0, 0)
    const3 = lambda bi, s, pt: (0, 0, 0)
    return pl.pallas_call(
        functools.partial(_paged_kernel, g_pages=g_pages, lam_init=lam_init),
        grid_spec=pltpu.PrefetchScalarGridSpec(
            num_scalar_prefetch=1, grid=(b, n_pages // g_pages),
            in_specs=[pl.BlockSpec((1, SAMPLE_ROWS, 3 * w), lambda bi, s, pt: (bi, 0, 0)),
                      pl.BlockSpec((1, N_KV_SLOTS * w, page), lambda bi, s, pt: (bi, 0, 0))]
                     + [page_spec(i) for i in range(g_pages)]
                     + [pl.BlockSpec(bm.shape, const3), pl.BlockSpec(bd.shape, const3),
                        pl.BlockSpec((page, page), const2), pl.BlockSpec(dl.shape, const2),
                        pl.BlockSpec((1, w), const2)],
            out_specs=pl.BlockSpec((1, SAMPLE_ROWS, 3 * w), lambda bi, s, pt: (bi, 0, 0)),
            scratch_shapes=[pltpu.VMEM((rows, w), BF16), pltpu.VMEM((rows, w), BF16),
                            pltpu.VMEM((2 * rows, w), BF16),
                            pltpu.VMEM((nblk + 1, rows, 1), F32), pltpu.VMEM((nblk + 1, rows, 1), F32),
                            pltpu.VMEM((nblk + 1, rows, 1), F32), pltpu.VMEM((nblk + 1, rows, w), F32),
                            pltpu.VMEM((rows, 1), F32), pltpu.VMEM((rows, w), F32),
                            pltpu.VMEM((2 * rows, 1), F32), pltpu.VMEM((2 * rows, 1), F32),
                            pltpu.VMEM((2 * rows, w), F32)]),
        out_shape=jax.ShapeDtypeStruct((b, SAMPLE_ROWS, 3 * w), F32),
        compiler_params=_cparams("parallel", "arbitrary"),
    )(page_table, q3, new_page, *([cache_t] * g_pages), bm, bd, _suffix_tri(page), dl, subln)


def _sample_bias(tab, page, ts):
    bucket = _bucket_table(2 * page + SAMPLE_ROWS)
    assert (bucket[page + 1:] == bucket[-1]).all()
    q = np.minimum(np.arange(SAMPLE_ROWS), ts - 1)
    k = np.arange(page)
    idx = np.stack([bucket[np.maximum(q[:, None] - k[None, :], 0)],
                    bucket[page + q[:, None] - k[None, :]],
                    np.full((SAMPLE_ROWS, page), bucket[-1])])
    b = jnp.transpose(tab[jnp.asarray(idx)], (0, 3, 1, 2))
    return b.reshape(3, N_HEADS * SAMPLE_ROWS, page)


def _post_kernel(ya_ref, yb_ref, yc_ref, yd_ref, x_ref, wo_ref, ln1_ref, wq_ref, mem_ref, wxo_ref, ln2_ref,
                 o_ref, *, alpha):
    w = GROUP_WIDTH
    mixed = None
    for i, y_ref in enumerate((ya_ref, yb_ref, yc_ref, yd_ref)):
        part = _dot(y_ref[0].astype(BF16), wo_ref[i * w:(i + 1) * w, :])
        mixed = part if mixed is None else mixed + part
    x1 = _layer_norm(alpha * x_ref[0] + mixed, ln1_ref[0:1], ln1_ref[1:2])
    qx = _dot(x1.astype(BF16), wq_ref[...])
    kt = mem_ref[0, 0, :w, :].astype(BF16)
    vt = mem_ref[0, 0, w:, :].astype(BF16)
    head = lax.broadcasted_iota(jnp.int32, qx.shape, 1) // HEAD_DIM
    o = jnp.zeros_like(qx)
    for h in range(N_HEADS):
        s = _dot(jnp.where(head == h, qx, 0.0).astype(BF16), kt) * HEAD_DIM ** -0.5
        p = jnp.exp(s - jnp.max(s, axis=1, keepdims=True))
        p = p / jnp.sum(p, axis=1, keepdims=True)
        o = jnp.where(head == h, _dot_nt(p.astype(BF16), vt), o)
    xo = _dot(o.astype(BF16), wxo_ref[...])
    o_ref[0] = _layer_norm(alpha * x1 + xo, ln2_ref[0:1], ln2_ref[1:2])


def _post(ys, x, w_out, ln1, w_xq, mem_kv_t, layer, w_xo, ln2, alpha, tm):
    b, t, d = x.shape
    w = GROUP_WIDTH
    tm = min(tm, t)
    m_len = mem_kv_t.shape[3]
    row = lambda width: pl.BlockSpec((1, tm, width), lambda i, j: (i, j, 0))
    const = lambda shape: pl.BlockSpec(shape, lambda i, j: (0,) * len(shape))
    return pl.pallas_call(
        functools.partial(_post_kernel, alpha=alpha),
        grid=(b, t // tm),
        in_specs=[row(w)] * 4 + [row(d), const(w_out.shape), const(ln1.shape), const(w_xq.shape),
                                 pl.BlockSpec((1, 1, 2 * w, m_len), lambda i, j: (i, layer, 0, 0)),
                                 const(w_xo.shape), const(ln2.shape)],
        out_specs=row(d),
        out_shape=jax.ShapeDtypeStruct((b, t, d), F32),
        compiler_params=_cparams("parallel", "parallel"),
    )(*ys, x, w_out, ln1, w_xq, mem_kv_t, w_xo, ln2)


def _route(x, wr_ref, br_ref):
    logits = _dot(x, wr_ref[...], precision=lax.Precision.HIGHEST) + br_ref[...]
    lane = lax.broadcasted_iota(jnp.int32, logits.shape, 1)
    lanef = lane.astype(F32)
    is_grp = jnp.logical_and(lane >= N_EXPERTS, lane < N_EXPERTS + N_GROUPS)
    lg = jnp.where(is_grp, logits, -jnp.inf)
    gmax = jnp.max(lg, axis=1, keepdims=True)
    pg_sel = 1.0 / jnp.sum(jnp.exp(lg - gmax), axis=1, keepdims=True)
    gsel = jnp.min(jnp.where(lg == gmax, lanef, float(LANES)), axis=1, keepdims=True) - N_EXPERTS
    in_grp = jnp.logical_and(lane < N_EXPERTS, (lane // EXPERTS_PER_GROUP).astype(F32) == gsel)
    le = jnp.where(in_grp, logits, -jnp.inf)
    pe = jnp.exp(le - jnp.max(le, axis=1, keepdims=True))
    pe = pe / jnp.sum(pe, axis=1, keepdims=True)
    cand = jnp.where(in_grp, pe, -1.0)
    p1 = jnp.max(cand, axis=1, keepdims=True)
    i1 = jnp.min(jnp.where(cand == p1, lanef, float(LANES)), axis=1, keepdims=True)
    cand = jnp.where(lanef == i1, -1.0, cand)
    p2 = jnp.max(cand, axis=1, keepdims=True)
    i2 = jnp.min(jnp.where(cand == p2, lanef, float(LANES)), axis=1, keepdims=True)
    top = jnp.where(lanef == i1, p1, 0.0) + jnp.where(lanef == i2, p2, 0.0)
    return pg_sel * top / (p1 + p2)


def _moe_kernel(x_ref, wr_ref, br_ref, wg_ref, wu_ref, wd_ref, ln_ref, o_ref, xb_ref, cw_ref, acc_ref, *, alpha):
    e = pl.program_id(1)

    @pl.when(e == 0)
    def _():
        x = x_ref[...]
        xb_ref[...] = x.astype(BF16)
        cw_ref[...] = _route(x, wr_ref, br_ref)
        acc_ref[...] = jnp.zeros_like(acc_ref)

    xb = xb_ref[...]
    gate = _dot(xb, wg_ref[0])
    hid = gate * _sigmoid(gate) * _dot(xb, wu_ref[0])
    y = _dot(hid.astype(BF16), wd_ref[0])
    lane = lax.broadcasted_iota(jnp.int32, cw_ref.shape, 1)
    cw = jnp.sum(jnp.where(lane == e, cw_ref[...], 0.0), axis=1, keepdims=True)
    acc_ref[...] += cw * y

    @pl.when(e == pl.num_programs(1) - 1)
    def _():
        o_ref[...] = _layer_norm(alpha * x_ref[...] + acc_ref[...], ln_ref[0:1], ln_ref[1:2])


def _moe(x, w_router, b_router, w_gate, w_up, w_down, ln3, alpha, tm):
    n, d = x.shape
    tm = min(tm, n)
    n_e, _, d_ff = w_gate.shape
    return pl.pallas_call(
        functools.partial(_moe_kernel, alpha=alpha),
        grid=(n // tm, n_e),
        in_specs=[pl.BlockSpec((tm, d), lambda i, e: (i, 0)),
                  pl.BlockSpec(w_router.shape, lambda i, e: (0, 0)),
                  pl.BlockSpec(b_router.shape, lambda i, e: (0, 0)),
                  pl.BlockSpec((1, d, d_ff), lambda i, e: (e, 0, 0)),
                  pl.BlockSpec((1, d, d_ff), lambda i, e: (e, 0, 0)),
                  pl.BlockSpec((1, d_ff, d), lambda i, e: (e, 0, 0)),
                  pl.BlockSpec(ln3.shape, lambda i, e: (0, 0))],
        out_specs=pl.BlockSpec((tm, d), lambda i, e: (i, 0)),
        out_shape=jax.ShapeDtypeStruct((n, d), F32),
        scratch_shapes=[pltpu.VMEM((tm, d), BF16), pltpu.VMEM((tm, LANES), F32), pltpu.VMEM((tm, d), F32)],
        compiler_params=_cparams("parallel", "arbitrary"),
    )(x, w_router, b_router, w_gate, w_up, w_down, ln3)


def _block_diag(wh):
    h, d, _ = wh.shape
    eye = jnp.eye(h, dtype=wh.dtype)
    return (eye[:, None, :, None] * wh[:, :, None, :]).reshape(h * d, h * d)


def _layer_params(l, p):
    w = GROUP_WIDTH
    w_in = p['w_in'][l]
    cols = lambda idx: jnp.concatenate([w_in[:, i * w:(i + 1) * w] for i in idx], axis=1).astype(BF16)
    d = w_in.shape[0]
    w_router = jnp.zeros((d, LANES), F32)
    w_router = w_router.at[:, :N_EXPERTS].set(p['w_re'][l]).at[:, N_EXPERTS:N_EXPERTS + N_GROUPS].set(p['w_rg'][l])
    b_router = jnp.zeros((1, LANES), F32)
    b_router = b_router.at[0, :N_EXPERTS].set(p['b_re'][l]).at[0, N_EXPERTS:N_EXPERTS + N_GROUPS].set(p['b_rg'][l])
    w_kv = cols((3, 4, 6, 7, 9, 10))
    return dict(
        w_q=cols((0, 1, 2, 5, 8)),
        w_kv=w_kv, w_kv_t=w_kv.T,
        pvec=jnp.concatenate([p['conv_b'][l][None], p['lru_ba'][l][None], p['lru_bx'][l][None],
                              p['lru_lambda'][l][None], p['conv_w'][l]], axis=0),
        w_gates=jnp.concatenate([_block_diag(p['lru_wa'][l]), _block_diag(p['lru_wx'][l])], axis=1).astype(BF16),
        dl=p['diff_lambda'][l],
        subln=jnp.tile(p['diff_subln'][l], N_HEADS)[None],
        lam_init=0.8 - 0.6 * math.exp(-0.3 * l),
        w_out=p['w_out'][l].astype(BF16),
        ln1=jnp.stack([p['ln1_g'][l], p['ln1_b'][l]]),
        w_xq=p['w_xq'][l].astype(BF16),
        w_xo=p['w_xo'][l].astype(BF16),
        ln2=jnp.stack([p['ln2_g'][l], p['ln2_b'][l]]),
        w_router=w_router, b_router=b_router,
        w_gate=p['w_gate'][l].astype(BF16), w_up=p['w_up'][l].astype(BF16), w_down=p['w_down'][l].astype(BF16),
        ln3=jnp.stack([p['ln3_g'][l], p['ln3_b'][l]]),
        w_xkv_t=p['w_xkv'][l].astype(BF16).T,
    )


def _pack_state(conv_buf, h0):
    b, _, w = conv_buf.shape
    pad = jnp.zeros((b, SUBLANES - CONV_WIDTH, w), F32)
    return jnp.concatenate([conv_buf, h0[:, None], pad], axis=1)


def _finish_layer(x, ys, mem_kv_t, layer, lp, alpha, tm):
    b, t, d = x.shape
    x2 = _post(ys, x, lp['w_out'], lp['ln1'], lp['w_xq'], mem_kv_t, layer, lp['w_xo'], lp['ln2'], alpha, tm)
    x3 = _moe(x2.reshape(b * t, d), lp['w_router'], lp['b_router'], lp['w_gate'], lp['w_up'], lp['w_down'],
              lp['ln3'], alpha, 512)
    return x3.reshape(b, t, d)


def _token_major(x_t, lead):
    n_slots = x_t.shape[-2] // GROUP_WIDTH
    x = x_t.reshape(x_t.shape[:lead] + (n_slots, N_HEADS, HEAD_DIM, x_t.shape[-1]))
    return jnp.moveaxis(x, -1, lead)


def kernel(x_prompt, x_sample, cache_kv, cache_mem_kv, state_rglru_h, state_conv, page_table, mem_prompt, rel_bias, w_in, conv_w, conv_b, lru_wa, lru_ba, lru_wx, lru_bx, lru_lambda, diff_lambda, diff_subln, w_out, ln1_g, ln1_b, w_xq, w_xkv, w_xo, ln2_g, ln2_b, w_rg, b_rg, w_re, b_re, w_gate, w_up, w_down, ln3_g, ln3_b):
    p = dict(w_in=w_in, conv_w=conv_w, conv_b=conv_b, lru_wa=lru_wa, lru_ba=lru_ba, lru_wx=lru_wx, lru_bx=lru_bx,
             lru_lambda=lru_lambda, diff_lambda=diff_lambda, diff_subln=diff_subln, w_out=w_out, ln1_g=ln1_g,
             ln1_b=ln1_b, w_xq=w_xq, w_xkv=w_xkv, w_xo=w_xo, ln2_g=ln2_g, ln2_b=ln2_b, w_rg=w_rg, b_rg=b_rg,
             w_re=w_re, b_re=b_re, w_gate=w_gate, w_up=w_up, w_down=w_down, ln3_g=ln3_g, ln3_b=ln3_b)
    depth = w_in.shape[0]
    alpha = (2 * depth) ** 0.25
    bp, tp, d = x_prompt.shape
    bs, ts, _ = x_sample.shape
    w = GROUP_WIDTH
    n_pool, _, page = cache_kv.shape[:3]
    assert tp % TQ == 0 and ts <= SAMPLE_ROWS and (page_table.shape[1] * page) % MOBA_BLOCK == 0
    m_len = mem_prompt.shape[1]
    cache_t = jnp.moveaxis(cache_kv, 2, -1).reshape(n_pool, depth, N_KV_SLOTS * w, page)
    mem_sample_t = jnp.moveaxis(cache_mem_kv, 2, -1).reshape(bs, depth, 2 * w, m_len)

    tab_moba, tab_diff = rel_bias[:, :N_HEADS], rel_bias[:, N_HEADS:]
    pb_moba, far_moba = _prompt_bias(tab_moba)
    pb_diff, far_diff = _prompt_bias(tab_diff)
    sb_moba = _sample_bias(tab_moba, page, ts)
    sb_diff = jnp.concatenate([_sample_bias(tab_diff, page, ts)] * 2, axis=1)

    xp = x_prompt
    xs = jnp.pad(x_sample, ((0, 0), (0, SAMPLE_ROWS - ts), (0, 0)))
    st_p = jnp.zeros((bp, SUBLANES, w), F32)
    outs = dict(kv_p=[], mem_p=[], h_p=[], conv_p=[], kv_s=[], h_s=[], conv_s=[])
    for l in range(depth):
        lp = _layer_params(l, p)
        mkv_t = _memory_kv_t(mem_prompt, lp['w_xkv_t'])

        qag, kvt, kvb = _project(xp, lp['w_q'], lp['w_kv_t'])
        y_a, st = _rglru(qag, st_p, lp['pvec'], lp['w_gates'], tc=TQ, tv=TQ)
        y_b, y_c, y_d = _prompt_mixers(qag, kvt, kvb, pb_moba, far_moba, pb_diff, far_diff,
                                       lp['dl'], lp['subln'], lp['lam_init'])
        xp = _finish_layer(xp, (y_a, y_b, y_c, y_d), mkv_t, 0, lp, alpha, 512)
        outs['kv_p'].append(kvt)
        outs['mem_p'].append(mkv_t[:, 0])
        outs['h_p'].append(st[:, CONV_WIDTH - 1])
        outs['conv_p'].append(st[:, :CONV_WIDTH - 1])

        qag, kv = _matmul(xs.reshape(bs * SAMPLE_ROWS, d), [lp['w_q'], lp['w_kv']])
        qag, kv = qag.reshape(bs, SAMPLE_ROWS, 5 * w), kv.reshape(bs, SAMPLE_ROWS, N_KV_SLOTS * w)
        y_a, st = _rglru(qag, _pack_state(state_conv[:, l], state_rglru_h[:, l]), lp['pvec'], lp['w_gates'],
                         tc=SAMPLE_ROWS, tv=ts)
        new_page = jnp.pad(jnp.swapaxes(kv, 1, 2), ((0, 0), (0, 0), (0, page - SAMPLE_ROWS)))
        y3 = _paged_attention(qag[:, :, 2 * w:], new_page, cache_t, page_table, l, sb_moba, sb_diff,
                              lp['dl'], lp['subln'], lp['lam_init'])
        xs = _finish_layer(xs, (y_a, y3[:, :, :w], y3[:, :, w:2 * w], y3[:, :, 2 * w:]), mem_sample_t, l,
                           lp, alpha, SAMPLE_ROWS)
        outs['kv_s'].append(kv[:, :ts].reshape(bs, ts, N_KV_SLOTS, N_HEADS, HEAD_DIM))
        outs['h_s'].append(st[:, CONV_WIDTH - 1])
        outs['conv_s'].append(st[:, :CONV_WIDTH - 1])

    stack = lambda k: jnp.stack(outs[k], axis=1)
    return (xp, xs[:, :ts], _token_major(stack('kv_p'), 2), _token_major(stack('mem_p'), 2), stack('h_p'),
            stack('conv_p'), stack('kv_s'), stack('h_s'), stack('conv_s'))
```

```python
import functools
import math

import numpy as np
import jax
import jax.numpy as jnp
from jax import lax
from jax.experimental import pallas as pl
from jax.experimental.pallas import tpu as pltpu

F32 = jnp.float32
BF16 = jnp.bfloat16

HEAD_DIM = 64
N_HEADS = 4
GROUP_WIDTH = N_HEADS * HEAD_DIM
N_KV_SLOTS = 6
CONV_WIDTH = 4
LRU_C = 8.0
MOBA_BLOCK = 256
MOBA_TOPK = 3
DIFF_HALF = HEAD_DIM // 2
N_BUCKETS = 32
MAX_EXACT = N_BUCKETS // 2
MAX_DISTANCE = 128
N_GROUPS = 4
EXPERTS_PER_GROUP = 4
N_EXPERTS = N_GROUPS * EXPERTS_PER_GROUP
LN_EPS = 1e-5
NEG = -1e30
LOG2E = math.log2(math.e)
LANES = 128
SUBLANES = 8
VMEM_LIMIT = 56 * 1024 * 1024
TQ = MOBA_BLOCK
Q_TILES = 2
SAMPLE_ROWS = SUBLANES
PAGES_PER_STEP = 8


def _cparams(*sem):
    return pltpu.CompilerParams(dimension_semantics=sem, vmem_limit_bytes=VMEM_LIMIT)


def _softplus(x):
    return jnp.maximum(x, 0.0) + jnp.log1p(jnp.exp(-jnp.abs(x)))


def _sigmoid(x):
    return 1.0 / (1.0 + jnp.exp(-x))


def _dot(a, b, precision=None):
    return jnp.dot(a, b, preferred_element_type=F32, precision=precision)


def _dot_nt(a, b):
    return lax.dot_general(a, b, (((1,), (1,)), ((), ())), preferred_element_type=F32)


def _layer_norm(x, g, b):
    mu = jnp.mean(x, axis=-1, keepdims=True)
    xc = x - mu
    var = jnp.mean(xc * xc, axis=-1, keepdims=True)
    return xc * lax.rsqrt(var + LN_EPS) * g + b


def _bucket_table(n):
    d = np.arange(n)
    large = MAX_EXACT + (np.log(np.maximum(d, 1).astype(np.float32) / MAX_EXACT)
                         / math.log(MAX_DISTANCE / MAX_EXACT) * (N_BUCKETS - MAX_EXACT)).astype(np.int32)
    return np.where(d < MAX_EXACT, d, np.minimum(large, N_BUCKETS - 1)).astype(np.int32)


def _mm_kernel(*refs, n_out):
    x = refs[0][...].astype(BF16)
    for w_ref, o_ref in zip(refs[1:1 + n_out], refs[1 + n_out:]):
        o_ref[...] = _dot(x, w_ref[...])


def _matmul(x, ws, tm=512):
    m, k = x.shape
    tm = min(tm, m)
    return pl.pallas_call(
        functools.partial(_mm_kernel, n_out=len(ws)),
        grid=(m // tm,),
        in_specs=[pl.BlockSpec((tm, k), lambda i: (i, 0))] + [pl.BlockSpec(w.shape, lambda i: (0, 0)) for w in ws],
        out_specs=[pl.BlockSpec((tm, w.shape[1]), lambda i: (i, 0)) for w in ws],
        out_shape=[jax.ShapeDtypeStruct((m, w.shape[1]), F32) for w in ws],
        compiler_params=_cparams("parallel"),
    )(x, *ws)


def _proj_kernel(x_ref, wq_ref, wkvt_ref, q_ref, kvt_ref, kvb_ref):
    x = x_ref[0].astype(BF16)
    q_ref[0] = _dot(x, wq_ref[...])
    kvt = _dot_nt(wkvt_ref[...], x)
    kvt_ref[0] = kvt
    kvb_ref[0] = kvt.astype(BF16)


def _project(x, w_q, w_kv_t, tm=512):
    b, t, d = x.shape
    tm = min(tm, t)
    nq, nkv = w_q.shape[1], w_kv_t.shape[0]
    return pl.pallas_call(
        _proj_kernel,
        grid=(b, t // tm),
        in_specs=[pl.BlockSpec((1, tm, d), lambda i, j: (i, j, 0)),
                  pl.BlockSpec(w_q.shape, lambda i, j: (0, 0)),
                  pl.BlockSpec(w_kv_t.shape, lambda i, j: (0, 0))],
        out_specs=[pl.BlockSpec((1, tm, nq), lambda i, j: (i, j, 0)),
                   pl.BlockSpec((1, nkv, tm), lambda i, j: (i, 0, j)),
                   pl.BlockSpec((1, nkv, tm), lambda i, j: (i, 0, j))],
        out_shape=[jax.ShapeDtypeStruct((b, t, nq), F32), jax.ShapeDtypeStruct((b, nkv, t), F32),
                   jax.ShapeDtypeStruct((b, nkv, t), BF16)],
        compiler_params=_cparams("parallel", "parallel"),
    )(x, w_q, w_kv_t)


def _mem_kernel(m_ref, w_ref, o_ref):
    o_ref[0, 0] = _dot_nt(w_ref[...], m_ref[0].astype(BF16))


def _memory_kv_t(mem, w_xkv_t):
    b, m, d = mem.shape
    n = w_xkv_t.shape[0]
    return pl.pallas_call(
        _mem_kernel, grid=(b,),
        in_specs=[pl.BlockSpec((1, m, d), lambda i: (i, 0, 0)), pl.BlockSpec(w_xkv_t.shape, lambda i: (0, 0))],
        out_specs=pl.BlockSpec((1, 1, n, m), lambda i: (i, 0, 0, 0)),
        out_shape=jax.ShapeDtypeStruct((b, 1, n, m), F32),
        compiler_params=_cparams("parallel"),
    )(mem, w_xkv_t)


def _rglru_kernel(x_ref, g_ref, st_ref, pv_ref, w_ref, y_ref, so_ref, carry_ref, *, tc, tv):
    c = pl.program_id(1)

    @pl.when(c == 0)
    def _():
        carry_ref[...] = st_ref[0]

    w = GROUP_WIDTH
    x = x_ref[0]
    row = lax.broadcasted_iota(jnp.int32, (tc, w), 0)
    pv = pv_ref[...]
    conv_b, ba, bx, lam = pv[0:1], pv[1:2], pv[2:3], pv[3:4]
    prev = carry_ref[...]
    u = conv_b + pv[4 + CONV_WIDTH - 1:4 + CONV_WIDTH] * x
    for sft in range(1, CONV_WIDTH):
        xs = pltpu.roll(x, sft, 0)
        for i in range(sft):
            xs = jnp.where(row == i, prev[CONV_WIDTH - 1 + i - sft:CONV_WIDTH + i - sft], xs)
        u = u + pv[4 + CONV_WIDTH - 1 - sft:4 + CONV_WIDTH - sft] * xs
    gates = _dot(u.astype(BF16), w_ref[...])
    r = _sigmoid(gates[:, :w] + ba)
    ig = _sigmoid(gates[:, w:] + bx)
    log_a = -LRU_C * r * _softplus(-lam)
    a = jnp.exp(log_a)
    b = jnp.sqrt(-jnp.tanh(log_a) * (jnp.exp(2.0 * log_a) + 1.0)) * ig * u
    s = 1
    while s < tc:
        a_s = pltpu.roll(a, s, 0)
        b_s = pltpu.roll(b, s, 0)
        m = row >= s
        b = jnp.where(m, a * b_s + b, b)
        a = jnp.where(m, a * a_s, a)
        s *= 2
    h = b + a * prev[CONV_WIDTH - 1:CONV_WIDTH]
    gt = g_ref[0]
    gelu = 0.5 * gt * (1.0 + jnp.tanh(math.sqrt(2.0 / math.pi) * (gt + 0.044715 * (gt * gt * gt))))
    y_ref[0] = h * gelu
    carry_ref[0:CONV_WIDTH - 1] = x[tv - (CONV_WIDTH - 1):tv]
    carry_ref[CONV_WIDTH - 1:CONV_WIDTH] = h[tv - 1:tv]

    @pl.when(c == pl.num_programs(1) - 1)
    def _():
        so_ref[0] = carry_ref[...]


def _rglru(qag, state, pvec, w_gates, tc, tv):
    b, t, _ = qag.shape
    w = GROUP_WIDTH
    assert t % tc == 0 and tv >= CONV_WIDTH - 1
    return pl.pallas_call(
        functools.partial(_rglru_kernel, tc=tc, tv=tv),
        grid=(b, t // tc),
        in_specs=[pl.BlockSpec((1, tc, w), lambda i, c: (i, c, 0)),
                  pl.BlockSpec((1, tc, w), lambda i, c: (i, c, 1)),
                  pl.BlockSpec((1, SUBLANES, w), lambda i, c: (i, 0, 0)),
                  pl.BlockSpec((SUBLANES, w), lambda i, c: (0, 0)),
                  pl.BlockSpec((w, 2 * w), lambda i, c: (0, 0))],
        out_specs=[pl.BlockSpec((1, tc, w), lambda i, c: (i, c, 0)),
                   pl.BlockSpec((1, SUBLANES, w), lambda i, c: (i, 0, 0))],
        out_shape=[jax.ShapeDtypeStruct((b, t, w), F32), jax.ShapeDtypeStruct((b, SUBLANES, w), F32)],
        scratch_shapes=[pltpu.VMEM((SUBLANES, w), F32)],
        compiler_params=_cparams("parallel", "arbitrary"),
    )(qag, qag, state, pvec, w_gates)


def _tri_tables(n_super):
    qs, ks = [], []
    for qs_i in range(n_super):
        for ki in range(Q_TILES * qs_i + Q_TILES - 1, -1, -1):
            qs.append(qs_i)
            ks.append(ki)
    return jnp.asarray(np.array(qs, np.int32)), jnp.asarray(np.array(ks, np.int32))


def _wide(x):
    return jnp.concatenate([x, x], axis=1)


def _values_with(vt, h, fill):
    rowh = lax.broadcasted_iota(jnp.int32, vt.shape, 0) // HEAD_DIM
    return jnp.where(rowh == h, vt, jnp.full((), fill, vt.dtype))


def _softmax_update(j, rows, t2, m_cand, vt_aug, m_ref, acc_ref, shift=None):
    m_old = m_ref[j, rows, :]
    m_new = jnp.maximum(m_old, m_cand)
    off = m_new if shift is None else m_new - shift
    p = jnp.exp2(t2 - _wide(off))
    acc_ref[j, rows, :] = jnp.exp2(m_old - m_new) * acc_ref[j, rows, :] + _dot_nt(p.astype(BF16), vt_aug)
    m_ref[j, rows, :] = m_new


def _normalised(acc):
    return acc / pltpu.roll(acc, HEAD_DIM, 1)


def _moba_select(q, km, q_blk):
    blk = lax.broadcasted_iota(jnp.int32, q_blk.shape, 1)
    blkf = blk.astype(F32)
    gate = _dot(q, km, precision=lax.Precision.HIGHEST)
    g = jnp.where(blk < q_blk, gate, -jnp.inf)
    sel = jnp.zeros(q_blk.shape, F32)
    for _ in range(MOBA_TOPK):
        mx = jnp.max(g, axis=1, keepdims=True)
        idx = jnp.min(jnp.where(g == mx, blkf, float(LANES)), axis=1, keepdims=True)
        hit = blkf == idx
        sel = jnp.where(hit, 1.0, sel)
        g = jnp.where(hit, -jnp.inf, g)
    return jnp.where(blk < q_blk, sel, 0.0)


def _diff_lambda(dl, lam_init):
    return (jnp.exp(jnp.sum(dl[0:1] * dl[1:2], axis=1, keepdims=True))
            - jnp.exp(jnp.sum(dl[2:3] * dl[3:4], axis=1, keepdims=True)) + lam_init)


def _mix_kernel(qt_ref, kt_ref, qb_ref, qc_ref, qd_ref, kv_ref,
                ntri_ref, km_ref, bias_b_ref, far_b_ref, bias_d_ref, far_d_ref, dl_ref, sub_ref,
                ob_ref, oc_ref, od_ref,
                qmb_ref, sel_ref, mb_ref, accb_ref, qmc_ref, run_ref, accc_ref, qmd_ref, md_ref, accd_ref,
                *, lam_init):
    t = pl.program_id(2)
    qs, ki = qt_ref[t], kt_ref[t]
    off = ki - Q_TILES * qs

    @pl.when(off == Q_TILES - 1)
    def _():
        qb = qb_ref[0]
        qb2 = qb * (HEAD_DIM ** -0.5 * LOG2E)
        qc2 = qc_ref[0] * (HEAD_DIM ** -0.5 * LOG2E)
        qd2 = qd_ref[0] * (DIFF_HALF ** -0.5 * LOG2E)
        q_blk = Q_TILES * qs + lax.broadcasted_iota(jnp.int32, (Q_TILES * TQ, LANES), 0) // TQ
        for h in range(2):
            hs = slice(h * HEAD_DIM, (h + 1) * HEAD_DIM)
            sel_ref[:, h * LANES:(h + 1) * LANES] = _moba_select(qb[:, hs], km_ref[0, hs, :], q_blk).astype(BF16)
            qmb_ref[h] = qb2[:, hs].astype(BF16)
            qmc_ref[h] = qc2[:, hs].astype(BF16)
        for j in range(4):
            qmd_ref[j] = qd2[:, j * DIFF_HALF:(j + 1) * DIFF_HALF].astype(BF16)
        mb_ref[...] = jnp.full_like(mb_ref, NEG)
        md_ref[...] = jnp.full_like(md_ref, NEG)
        accb_ref[...] = jnp.zeros_like(accb_ref)
        accc_ref[...] = jnp.zeros_like(accc_ref)
        accd_ref[...] = jnp.zeros_like(accd_ref)
        run_ref[...] = jnp.zeros_like(run_ref)


    def moba_step(rows, dist):
        kt = kv_ref[0, 0, 0]
        vt = kv_ref[0, 1, 0]
        if dist != 0:
            hit = lax.broadcasted_iota(jnp.int32, (LANES, LANES), 0) == ki
            e = jnp.where(hit, 1.0, 0.0).astype(BF16)
            z = jnp.zeros_like(e)
            onehot = jnp.concatenate([jnp.concatenate([e, z], axis=1), jnp.concatenate([z, e], axis=1)], axis=0)
            picked = _dot(sel_ref[rows, :], onehot)
        for h in range(2):
            raw = _dot(qmb_ref[h, rows, :], kt[h * HEAD_DIM:(h + 1) * HEAD_DIM])
            vt_aug = _values_with(vt, h, 1.0)
            row_pen = None if dist == 0 else jnp.where(picked[:, h * LANES:(h + 1) * LANES] > 0.5, 0.0, NEG)
            if dist is not None:
                t2 = raw + bias_b_ref[h, dist]
                m_cand = jnp.max(t2, axis=1, keepdims=True)
                m_cand = m_cand if dist == 0 else m_cand + row_pen
                _softmax_update(h, rows, t2, m_cand, vt_aug, mb_ref, accb_ref, shift=row_pen)
            else:
                shift = far_b_ref[h, 0:1] + row_pen
                m_cand = jnp.max(raw, axis=1, keepdims=True) + shift
                _softmax_update(h, rows, raw, m_cand, vt_aug, mb_ref, accb_ref, shift=shift)

    def stick_step(rows, dist):
        kt = kv_ref[0, 2, 0]
        vt = kv_ref[0, 3, 0]
        n = rows.stop - rows.start
        if dist == 0:
            r = lax.broadcasted_iota(jnp.int32, (TQ, TQ), 0)
            c = lax.broadcasted_iota(jnp.int32, (TQ, TQ), 1)
            valid = c < r
        z2s, runs, parts = [], [], []
        for h in range(2):
            z2 = _dot(qmc_ref[h, rows, :], kt[h * HEAD_DIM:(h + 1) * HEAD_DIM])
            sp2 = jnp.maximum(z2, 0.0) + jnp.log2(1.0 + jnp.exp2(-jnp.abs(z2)))
            if dist == 0:
                sp2 = jnp.where(valid, sp2, 0.0)
            z2s.append(z2)
            parts.append(sp2.astype(BF16))
            runs.append(run_ref[h, rows, :] - jnp.broadcast_to(jnp.sum(sp2, axis=1, keepdims=True), (n, LANES)))
        cums = _dot(jnp.concatenate(parts, axis=0), ntri_ref[...])
        acc = accc_ref[rows, :]
        for h in range(2):
            e = jnp.exp2(z2s[h] + cums[h * n:(h + 1) * n] + _wide(run_ref[h, rows, :]))
            if dist == 0:
                e = jnp.where(valid, e, 0.0)
            acc = acc + _dot_nt(e.astype(BF16), _values_with(vt, h, 0.0))
            run_ref[h, rows, :] = runs[h]
        accc_ref[rows, :] = acc

    def diff_step(rows, dist):
        kt = kv_ref[0, 4, 0]
        vt = kv_ref[0, 5, 0]
        vt_aug = [_values_with(vt, h, 1.0) for h in range(2)]
        for j in range(4):
            h = j // 2
            raw = _dot(qmd_ref[j, rows, :], kt[j * DIFF_HALF:(j + 1) * DIFF_HALF])
            if dist is not None:
                t2 = raw + bias_d_ref[h, dist]
                _softmax_update(j, rows, t2, jnp.max(t2, axis=1, keepdims=True), vt_aug[h], md_ref, accd_ref)
            else:
                shift = far_d_ref[h, 0:1]
                m_cand = jnp.max(raw, axis=1, keepdims=True) + shift
                _softmax_update(j, rows, raw, m_cand, vt_aug[h], md_ref, accd_ref, shift=shift)

    def all_mixers(rows, dist):
        moba_step(rows, dist)
        stick_step(rows, dist)
        diff_step(rows, dist)

    def near_diagonal_step(o):
        for r in range(max(o, 0), min(o + 2, Q_TILES)):
            all_mixers(slice(r * TQ, (r + 1) * TQ), r - o)
        if o + 2 < Q_TILES:
            all_mixers(slice((o + 2) * TQ, Q_TILES * TQ), None)

    for o in range(Q_TILES - 1, -2, -1):
        pl.when(off == o)(functools.partial(near_diagonal_step, o))

    @pl.when(off <= -2)
    def _():
        all_mixers(slice(0, Q_TILES * TQ), None)

    @pl.when(ki == 0)
    def _():
        head = lax.broadcasted_iota(jnp.int32, (Q_TILES * TQ, LANES), 1) // HEAD_DIM
        ob_ref[0] = jnp.where(head == 0, _normalised(accb_ref[0]), _normalised(accb_ref[1]))
        oc_ref[0] = accc_ref[...]
        lam = _diff_lambda(dl_ref[...], lam_init)
        o = jnp.where(head == 0,
                      _normalised(accd_ref[0]) - lam * _normalised(accd_ref[1]),
                      _normalised(accd_ref[2]) - lam * _normalised(accd_ref[3]))
        o2 = o * o
        ms = jnp.where(head == 0,
                       jnp.sum(jnp.where(head == 0, o2, 0.0), axis=1, keepdims=True),
                       jnp.sum(jnp.where(head == 1, o2, 0.0), axis=1, keepdims=True)) * (1.0 / HEAD_DIM)
        od_ref[0] = o * lax.rsqrt(ms + LN_EPS) * sub_ref[...] * (1.0 - lam_init)


def _kmean_kernel(k_ref, ones_ref, o_ref):
    o_ref[0] = _dot(k_ref[0], ones_ref[...], precision=lax.Precision.HIGHEST)


def _suffix_tri(n, sign=1.0):
    i = np.arange(n)
    return jnp.asarray(sign * (i[:, None] >= i[None, :]).astype(np.float32), dtype=BF16)


def _prompt_mixers(qag, kvt, kvb, bias_b, far_b, bias_d, far_d, dl, subln, lam_init):
    b, t, _ = qag.shape
    kv5 = kvb.reshape(b, N_KV_SLOTS, 2, LANES, t)
    nb = t // MOBA_BLOCK
    assert nb <= LANES
    blk_mean = np.zeros((t, LANES), np.float32)
    blk_mean[np.arange(t), np.arange(t) // MOBA_BLOCK] = 1.0 / MOBA_BLOCK
    kmean_t = pl.pallas_call(
        _kmean_kernel, grid=(b,),
        in_specs=[pl.BlockSpec((1, GROUP_WIDTH, t), lambda i: (i, 0, 0)),
                  pl.BlockSpec((t, LANES), lambda i: (0, 0))],
        out_specs=pl.BlockSpec((1, GROUP_WIDTH, LANES), lambda i: (i, 0, 0)),
        out_shape=jax.ShapeDtypeStruct((b, GROUP_WIDTH, LANES), F32),
        compiler_params=_cparams("parallel"),
    )(kvt, jnp.asarray(blk_mean))
    rows = Q_TILES * TQ
    assert t % rows == 0
    qt, kt = _tri_tables(t // rows)
    q_spec = lambda col: pl.BlockSpec((1, rows, LANES), lambda b, p, t, qt, kt: (b, qt[t], col + p))
    kv_spec = pl.BlockSpec((1, N_KV_SLOTS, 1, LANES, TQ), lambda b, p, t, qt, kt: (b, 0, p, 0, kt[t]))
    bias_spec = pl.BlockSpec((2, 2, TQ, TQ), lambda b, p, t, qt, kt: (p, 0, 0, 0))
    far_spec = pl.BlockSpec((2, SUBLANES, LANES), lambda b, p, t, qt, kt: (p, 0, 0))
    const2 = lambda shape: pl.BlockSpec(shape, lambda b, p, t, qt, kt: (0, 0))
    out_spec = pl.BlockSpec((1, rows, LANES), lambda b, p, t, qt, kt: (b, qt[t], p))
    out = jax.ShapeDtypeStruct((b, t, GROUP_WIDTH), F32)
    stats = lambda n: pltpu.VMEM((n, rows, LANES), F32)
    return pl.pallas_call(
        functools.partial(_mix_kernel, lam_init=lam_init),
        grid_spec=pltpu.PrefetchScalarGridSpec(
            num_scalar_prefetch=2, grid=(b, 2, qt.shape[0]),
            in_specs=[q_spec(4), q_spec(6), q_spec(8), kv_spec]
                     + [const2((TQ, TQ)), pl.BlockSpec((1, LANES, LANES), lambda b, p, t, qt, kt: (b, p, 0)),
                        bias_spec, far_spec, bias_spec, far_spec, const2(dl.shape), const2((1, LANES))],
            out_specs=[out_spec] * 3,
            scratch_shapes=[pltpu.VMEM((2, rows, HEAD_DIM), BF16), pltpu.VMEM((rows, 2 * LANES), BF16),
                            stats(2), stats(2),
                            pltpu.VMEM((2, rows, HEAD_DIM), BF16), stats(2), pltpu.VMEM((rows, LANES), F32),
                            pltpu.VMEM((4, rows, DIFF_HALF), BF16), stats(4), stats(4)]),
        out_shape=[out] * 3,
        compiler_params=_cparams("parallel", "parallel", "arbitrary"),
    )(qt, kt, qag, qag, qag, kv5, _suffix_tri(TQ, -1.0), kmean_t,
      bias_b, far_b, bias_d, far_d, dl, subln[:, :LANES])


def _toeplitz(g):
    h, l = g.shape
    n = l // 2
    w = jnp.concatenate([g[:, :1], jnp.flip(g[:, 1:], axis=1)], axis=1)
    rep = jnp.tile(w, (1, n))[:, :n * (l - 1)].reshape(h, n, l - 1)
    return rep[:, :, :n]


def _prompt_bias(tab):
    n = TQ
    bucket = _bucket_table(4 * n)
    assert (bucket[n + 1:] == bucket[-1]).all()
    x = np.arange(2 * n)
    d = np.where(x < n, x, x - 2 * n)
    fvec = (tab * LOG2E).T
    g0 = jnp.where(jnp.asarray(d >= 0), fvec[:, bucket[np.maximum(d, 0)]], NEG)
    g1 = fvec[:, bucket[n + d]]
    tiles = jnp.stack([_toeplitz(g0), _toeplitz(g1)], axis=1)
    far = jnp.broadcast_to(fvec[:, bucket[-1]][:, None, None], (tab.shape[1], SUBLANES, LANES))
    return tiles, far


def _expand_rows(x4):
    x = jnp.concatenate([x4] * N_HEADS, axis=0)
    rh = lax.broadcasted_iota(jnp.int32, x.shape, 0) // SAMPLE_ROWS
    lh = lax.broadcasted_iota(jnp.int32, x.shape, 1) // HEAD_DIM
    return jnp.where(rh == lh, x, 0.0)


def _collapse_rows(x):
    rh = lax.broadcasted_iota(jnp.int32, x.shape, 0) // SAMPLE_ROWS
    lh = lax.broadcasted_iota(jnp.int32, x.shape, 1) // HEAD_DIM
    x = jnp.where(rh == lh, x, 0.0)
    out = x[0:SAMPLE_ROWS]
    for h in range(1, N_HEADS):
        out = out + x[h * SAMPLE_ROWS:(h + 1) * SAMPLE_ROWS]
    return out


def _paged_kernel(pt_ref, q_ref, new_ref, *rest, g_pages, lam_init):
    page_refs = rest[:g_pages]
    bm_ref, bd_ref, tri_ref, dl_ref, sub_ref, o_ref = rest[g_pages:g_pages + 6]
    (qb_ref, qc_ref, qd_ref, mb_m, mb_l, mb_g, mb_acc,
     run_ref, accc_ref, md_ref, ld_ref, accd_ref) = rest[g_pages + 6:]
    s_idx = pl.program_id(1)
    n_steps = pl.num_programs(1)
    nblk = mb_m.shape[0] - 1
    w = GROUP_WIDTH
    rows = N_HEADS * SAMPLE_ROWS
    page = new_ref.shape[2]
    qq = lax.broadcasted_iota(jnp.int32, (rows, page), 0) % SAMPLE_ROWS
    kk = lax.broadcasted_iota(jnp.int32, (rows, page), 1)
    tri = tri_ref[...]

    def slot(pg, i):
        return pg[i * w:(i + 1) * w, :].astype(BF16)

    def moba_block(pages, biases, pens, n):
        ss, gsum = [], None
        for pg, bias, pen in zip(pages, biases, pens):
            raw = _dot(qb_ref[...], slot(pg, 0))
            sc = raw + bias
            ss.append(sc if pen is None else sc + pen)
            gsum = raw if gsum is None else gsum + raw
        m = ss[0].max(axis=1, keepdims=True)
        for sc in ss[1:]:
            m = jnp.maximum(m, sc.max(axis=1, keepdims=True))
        psum, acc = None, None
        for pg, sc in zip(pages, ss):
            p = jnp.exp(sc - m)
            pa = _dot_nt(p.astype(BF16), slot(pg, 1))
            psum = p if psum is None else psum + p
            acc = pa if acc is None else acc + pa
        mb_m[n] = m
        mb_l[n] = jnp.sum(psum, axis=1, keepdims=True)
        mb_acc[n] = acc
        mb_g[n] = jnp.sum(gsum, axis=1, keepdims=True)

    def stick_page(pg, valid):
        z = _dot(qc_ref[...], slot(pg, 2))
        lk = -_softplus(z)
        if valid is not None:
            lk = jnp.where(valid, lk, 0.0)
        hi = lk.astype(BF16)
        lo = (lk - hi.astype(F32)).astype(BF16)
        cum = _dot(hi, tri) + _dot(lo, tri)
        e = jnp.exp(z + cum + run_ref[...])
        if valid is not None:
            e = jnp.where(valid, e, 0.0)
        accc_ref[...] += _dot_nt(e.astype(BF16), slot(pg, 3))
        run_ref[...] += jnp.sum(lk, axis=1, keepdims=True)

    def diff_pages(pages, biases, pens):
        ss = []
        for pg, bias, pen in zip(pages, biases, pens):
            sc = _dot(qd_ref[...], slot(pg, 4)) * DIFF_HALF ** -0.5 + bias
            ss.append(sc if pen is None else sc + pen)
        mx = ss[0]
        for sc in ss[1:]:
            mx = jnp.maximum(mx, sc)
        m_old = md_ref[...]
        m_new = jnp.maximum(m_old, mx.max(axis=1, keepdims=True))
        alpha = jnp.exp(m_old - m_new)
        psum, acc = None, None
        for pg, sc in zip(pages, ss):
            p = jnp.exp(sc - m_new)
            pa = _dot_nt(p.astype(BF16), slot(pg, 5))
            psum = p if psum is None else psum + p
            acc = pa if acc is None else acc + pa
        ld_ref[...] = alpha * ld_ref[...] + jnp.sum(psum, axis=1, keepdims=True)
        accd_ref[...] = alpha * accd_ref[...] + acc
        md_ref[...] = m_new

    @pl.when(s_idx == 0)
    def _():
        q = q_ref[0]
        qb_ref[...] = (_expand_rows(q[:, 0:w]) * HEAD_DIM ** -0.5).astype(BF16)
        qc_ref[...] = (_expand_rows(q[:, w:2 * w]) * HEAD_DIM ** -0.5).astype(BF16)
        qdx = _expand_rows(q[:, 2 * w:3 * w])
        half = lax.broadcasted_iota(jnp.int32, qdx.shape, 1) // DIFF_HALF % 2
        qd_ref[0:rows] = jnp.where(half == 0, qdx, 0.0).astype(BF16)
        qd_ref[rows:2 * rows] = jnp.where(half == 1, qdx, 0.0).astype(BF16)
        run_ref[...] = jnp.zeros_like(run_ref)
        accc_ref[...] = jnp.zeros_like(accc_ref)
        md_ref[...] = jnp.full_like(md_ref, NEG)
        ld_ref[...] = jnp.zeros_like(ld_ref)
        accd_ref[...] = jnp.zeros_like(accd_ref)
        new = new_ref[0]
        causal_pen = jnp.where(kk <= qq, 0.0, NEG)
        moba_block([new], [bm_ref[0]], [causal_pen], nblk)
        stick_page(new, kk < qq)
        diff_pages([new], [bd_ref[0]], [jnp.concatenate([causal_pen, causal_pen], axis=0)])

    pages = [r[0, 0] for r in page_refs]
    bm0 = jnp.where(s_idx == 0, bm_ref[1], bm_ref[2])
    bd0 = jnp.where(s_idx == 0, bd_ref[1], bd_ref[2])
    bms = [bm0] + [bm_ref[2]] * (g_pages - 1)
    bds = [bd0] + [bd_ref[2]] * (g_pages - 1)
    per_block = MOBA_BLOCK // page
    for i in range(0, g_pages, per_block):
        n = nblk - 1 - (s_idx * g_pages + i) // per_block
        moba_block(pages[i:i + per_block], bms[i:i + per_block], [None] * per_block, n)
    for pg in pages:
        stick_page(pg, None)
    diff_pages(pages, bds, [None] * g_pages)

    @pl.when(s_idx == n_steps - 1)
    def _():
        blkf = lax.broadcasted_iota(jnp.int32, (nblk, rows, 1), 0).astype(F32)
        g = mb_g[0:nblk]
        sel = jnp.zeros((nblk, rows, 1), F32)
        for _ in range(min(MOBA_TOPK, nblk)):
            mx = jnp.max(g, axis=0, keepdims=True)
            idx = jnp.min(jnp.where(g == mx, blkf, float(nblk)), axis=0, keepdims=True)
            hit = blkf == idx
            sel = jnp.where(hit, 1.0, sel)
            g = jnp.where(hit, -jnp.inf, g)
        m_own = mb_m[nblk]
        m_tot = jnp.maximum(m_own, jnp.max(jnp.where(sel > 0.5, mb_m[0:nblk], NEG), axis=0))
        wgt = jnp.where(sel > 0.5, jnp.exp(mb_m[0:nblk] - m_tot), 0.0)
        w_own = jnp.exp(m_own - m_tot)
        l_tot = w_own * mb_l[nblk] + jnp.sum(wgt * mb_l[0:nblk], axis=0)
        a_tot = w_own * mb_acc[nblk] + jnp.sum(wgt * mb_acc[0:nblk], axis=0)
        o_ref[0, :, 0:w] = _collapse_rows(a_tot / l_tot)
        o_ref[0, :, w:2 * w] = _collapse_rows(accc_ref[...])
        lam = _diff_lambda(dl_ref[...], lam_init)
        od = (accd_ref[0:rows] / ld_ref[0:rows] - lam * (accd_ref[rows:2 * rows] / ld_ref[rows:2 * rows]))
        od = _collapse_rows(od)
        od2 = od * od
        lh = lax.broadcasted_iota(jnp.int32, od.shape, 1) // HEAD_DIM
        ms = jnp.zeros_like(od)
        for h in range(N_HEADS):
            ms = jnp.where(lh == h, jnp.sum(jnp.where(lh == h, od2, 0.0), axis=1, keepdims=True), ms)
        o_ref[0, :, 2 * w:3 * w] = od * lax.rsqrt(ms * (1.0 / HEAD_DIM) + LN_EPS) * sub_ref[...] * (1.0 - lam_init)


def _paged_attention(q3, new_page, cache_t, page_table, layer, bm, bd, dl, subln, lam_init):
    b = q3.shape[0]
    n_pages = page_table.shape[1]
    page = cache_t.shape[3]
    g_pages = min(PAGES_PER_STEP, n_pages)
    assert n_pages % g_pages == 0 and MOBA_BLOCK % page == 0 and g_pages % (MOBA_BLOCK // page) == 0
    nblk = n_pages * page // MOBA_BLOCK
    rows = N_HEADS * SAMPLE_ROWS
    w = GROUP_WIDTH

    def page_spec(i):
        return pl.BlockSpec((1, 1, N_KV_SLOTS * w, page),
                            lambda bi, s, pt: (pt[bi, n_pages - 1 - (s * g_pages + i)], layer, 0, 0))

    const2 = lambda bi, s, pt: (0, 0)
    const3 = lambda bi, s, pt: (0, 0, 0)
    return pl.pallas_call(
        functools.partial(_paged_kernel, g_pages=g_pages, lam_init=lam_init),
        grid_spec=pltpu.PrefetchScalarGridSpec(
            num_scalar_prefetch=1, grid=(b, n_pages // g_pages),
            in_specs=[pl.BlockSpec((1, SAMPLE_ROWS, 3 * w), lambda bi, s, pt: (bi, 0, 0)),
                      pl.BlockSpec((1, N_KV_SLOTS * w, page), lambda bi, s, pt: (bi, 0, 0))]
                     + [page_spec(i) for i in range(g_pages)]
                     + [pl.BlockSpec(bm.shape, const3), pl.BlockSpec(bd.shape, const3),
                        pl.BlockSpec((page, page), const2), pl.BlockSpec(dl.shape, const2),
                        pl.BlockSpec((1, w), const2)],
            out_specs=pl.BlockSpec((1, SAMPLE_ROWS, 3 * w), lambda bi, s, pt: (bi, 0, 0)),
            scratch_shapes=[pltpu.VMEM((rows, w), BF16), pltpu.VMEM((rows, w), BF16),
                            pltpu.VMEM((2 * rows, w), BF16),
                            pltpu.VMEM((nblk + 1, rows, 1), F32), pltpu.VMEM((nblk + 1, rows, 1), F32),
                            pltpu.VMEM((nblk + 1, rows, 1), F32), pltpu.VMEM((nblk + 1, rows, w), F32),
                            pltpu.VMEM((rows, 1), F32), pltpu.VMEM((rows, w), F32),
                            pltpu.VMEM((2 * rows, 1), F32), pltpu.VMEM((2 * rows, 1), F32),
                            pltpu.VMEM((2 * rows, w), F32)]),
        out_shape=jax.ShapeDtypeStruct((b, SAMPLE_ROWS, 3 * w), F32),
        compiler_params=_cparams("parallel", "arbitrary"),
    )(page_table, q3, new_page, *([cache_t] * g_pages), bm, bd, _suffix_tri(page), dl, subln)


def _sample_bias(tab, page, ts):
    bucket = _bucket_table(2 * page + SAMPLE_ROWS)
    assert (bucket[page + 1:] == bucket[-1]).all()
    q = np.minimum(np.arange(SAMPLE_ROWS), ts - 1)
    k = np.arange(page)
    idx = np.stack([bucket[np.maximum(q[:, None] - k[None, :], 0)],
                    bucket[page + q[:, None] - k[None, :]],
                    np.full((SAMPLE_ROWS, page), bucket[-1])])
    b = jnp.transpose(tab[jnp.asarray(idx)], (0, 3, 1, 2))
    return b.reshape(3, N_HEADS * SAMPLE_ROWS, page)


def _post_kernel(ya_ref, yb_ref, yc_ref, yd_ref, x_ref, wo_ref, ln1_ref, wq_ref, mem_ref, wxo_ref, ln2_ref,
                 o_ref, *, alpha):
    w = GROUP_WIDTH
    mixed = None
    for i, y_ref in enumerate((ya_ref, yb_ref, yc_ref, yd_ref)):
        part = _dot(y_ref[0].astype(BF16), wo_ref[i * w:(i + 1) * w, :])
        mixed = part if mixed is None else mixed + part
    x1 = _layer_norm(alpha * x_ref[0] + mixed, ln1_ref[0:1], ln1_ref[1:2])
    qx = _dot(x1.astype(BF16), wq_ref[...])
    kt = mem_ref[0, 0, :w, :].astype(BF16)
    vt = mem_ref[0, 0, w:, :].astype(BF16)
    head = lax.broadcasted_iota(jnp.int32, qx.shape, 1) // HEAD_DIM
    o = jnp.zeros_like(qx)
    for h in range(N_HEADS):
        s = _dot(jnp.where(head == h, qx, 0.0).astype(BF16), kt) * HEAD_DIM ** -0.5
        p = jnp.exp(s - jnp.max(s, axis=1, keepdims=True))
        p = p / jnp.sum(p, axis=1, keepdims=True)
        o = jnp.where(head == h, _dot_nt(p.astype(BF16), vt), o)
    xo = _dot(o.astype(BF16), wxo_ref[...])
    o_ref[0] = _layer_norm(alpha * x1 + xo, ln2_ref[0:1], ln2_ref[1:2])


def _post(ys, x, w_out, ln1, w_xq, mem_kv_t, layer, w_xo, ln2, alpha, tm):
    b, t, d = x.shape
    w = GROUP_WIDTH
    tm = min(tm, t)
    m_len = mem_kv_t.shape[3]
    row = lambda width: pl.BlockSpec((1, tm, width), lambda i, j: (i, j, 0))
    const = lambda shape: pl.BlockSpec(shape, lambda i, j: (0,) * len(shape))
    return pl.pallas_call(
        functools.partial(_post_kernel, alpha=alpha),
        grid=(b, t // tm),
        in_specs=[row(w)] * 4 + [row(d), const(w_out.shape), const(ln1.shape), const(w_xq.shape),
                                 pl.BlockSpec((1, 1, 2 * w, m_len), lambda i, j: (i, layer, 0, 0)),
                                 const(w_xo.shape), const(ln2.shape)],
        out_specs=row(d),
        out_shape=jax.ShapeDtypeStruct((b, t, d), F32),
        compiler_params=_cparams("parallel", "parallel"),
    )(*ys, x, w_out, ln1, w_xq, mem_kv_t, w_xo, ln2)


def _route(x, wr_ref, br_ref):
    logits = _dot(x, wr_ref[...], precision=lax.Precision.HIGHEST) + br_ref[...]
    lane = lax.broadcasted_iota(jnp.int32, logits.shape, 1)
    lanef = lane.astype(F32)
    is_grp = jnp.logical_and(lane >= N_EXPERTS, lane < N_EXPERTS + N_GROUPS)
    lg = jnp.where(is_grp, logits, -jnp.inf)
    gmax = jnp.max(lg, axis=1, keepdims=True)
    pg_sel = 1.0 / jnp.sum(jnp.exp(lg - gmax), axis=1, keepdims=True)
    gsel = jnp.min(jnp.where(lg == gmax, lanef, float(LANES)), axis=1, keepdims=True) - N_EXPERTS
    in_grp = jnp.logical_and(lane < N_EXPERTS, (lane // EXPERTS_PER_GROUP).astype(F32) == gsel)
    le = jnp.where(in_grp, logits, -jnp.inf)
    pe = jnp.exp(le - jnp.max(le, axis=1, keepdims=True))
    pe = pe / jnp.sum(pe, axis=1, keepdims=True)
    cand = jnp.where(in_grp, pe, -1.0)
    p1 = jnp.max(cand, axis=1, keepdims=True)
    i1 = jnp.min(jnp.where(cand == p1, lanef, float(LANES)), axis=1, keepdims=True)
    cand = jnp.where(lanef == i1, -1.0, cand)
    p2 = jnp.max(cand, axis=1, keepdims=True)
    i2 = jnp.min(jnp.where(cand == p2, lanef, float(LANES)), axis=1, keepdims=True)
    top = jnp.where(lanef == i1, p1, 0.0) + jnp.where(lanef == i2, p2, 0.0)
    return pg_sel * top / (p1 + p2)


def _moe_kernel(x_ref, wr_ref, br_ref, wg_ref, wu_ref, wd_ref, ln_ref, o_ref, xb_ref, cw_ref, acc_ref, *, alpha):
    e = pl.program_id(1)

    @pl.when(e == 0)
    def _():
        x = x_ref[...]
        xb_ref[...] = x.astype(BF16)
        cw_ref[...] = _route(x, wr_ref, br_ref)
        acc_ref[...] = jnp.zeros_like(acc_ref)

    xb = xb_ref[...]
    gate = _dot(xb, wg_ref[0])
    hid = gate * _sigmoid(gate) * _dot(xb, wu_ref[0])
    y = _dot(hid.astype(BF16), wd_ref[0])
    lane = lax.broadcasted_iota(jnp.int32, cw_ref.shape, 1)
    cw = jnp.sum(jnp.where(lane == e, cw_ref[...], 0.0), axis=1, keepdims=True)
    acc_ref[...] += cw * y

    @pl.when(e == pl.num_programs(1) - 1)
    def _():
        o_ref[...] = _layer_norm(alpha * x_ref[...] + acc_ref[...], ln_ref[0:1], ln_ref[1:2])


def _moe(x, w_router, b_router, w_gate, w_up, w_down, ln3, alpha, tm):
    n, d = x.shape
    tm = min(tm, n)
    n_e, _, d_ff = w_gate.shape
    return pl.pallas_call(
        functools.partial(_moe_kernel, alpha=alpha),
        grid=(n // tm, n_e),
        in_specs=[pl.BlockSpec((tm, d), lambda i, e: (i, 0)),
                  pl.BlockSpec(w_router.shape, lambda i, e: (0, 0)),
                  pl.BlockSpec(b_router.shape, lambda i, e: (0, 0)),
                  pl.BlockSpec((1, d, d_ff), lambda i, e: (e, 0, 0)),
                  pl.BlockSpec((1, d, d_ff), lambda i, e: (e, 0, 0)),
                  pl.BlockSpec((1, d_ff, d), lambda i, e: (e, 0, 0)),
                  pl.BlockSpec(ln3.shape, lambda i, e: (0, 0))],
        out_specs=pl.BlockSpec((tm, d), lambda i, e: (i, 0)),
        out_shape=jax.ShapeDtypeStruct((n, d), F32),
        scratch_shapes=[pltpu.VMEM((tm, d), BF16), pltpu.VMEM((tm, LANES), F32), pltpu.VMEM((tm, d), F32)],
        compiler_params=_cparams("parallel", "arbitrary"),
    )(x, w_router, b_router, w_gate, w_up, w_down, ln3)


def _block_diag(wh):
    h, d, _ = wh.shape
    eye = jnp.eye(h, dtype=wh.dtype)
    return (eye[:, None, :, None] * wh[:, :, None, :]).reshape(h * d, h * d)


def _layer_params(l, p):
    w = GROUP_WIDTH
    w_in = p['w_in'][l]
    cols = lambda idx: jnp.concatenate([w_in[:, i * w:(i + 1) * w] for i in idx], axis=1).astype(BF16)
    d = w_in.shape[0]
    w_router = jnp.zeros((d, LANES), F32)
    w_router = w_router.at[:, :N_EXPERTS].set(p['w_re'][l]).at[:, N_EXPERTS:N_EXPERTS + N_GROUPS].set(p['w_rg'][l])
    b_router = jnp.zeros((1, LANES), F32)
    b_router = b_router.at[0, :N_EXPERTS].set(p['b_re'][l]).at[0, N_EXPERTS:N_EXPERTS + N_GROUPS].set(p['b_rg'][l])
    w_kv = cols((3, 4, 6, 7, 9, 10))
    return dict(
        w_q=cols((0, 1, 2, 5, 8)),
        w_kv=w_kv, w_kv_t=w_kv.T,
        pvec=jnp.concatenate([p['conv_b'][l][None], p['lru_ba'][l][None], p['lru_bx'][l][None],
                              p['lru_lambda'][l][None], p['conv_w'][l]], axis=0),
        w_gates=jnp.concatenate([_block_diag(p['lru_wa'][l]), _block_diag(p['lru_wx'][l])], axis=1).astype(BF16),
        dl=p['diff_lambda'][l],
        subln=jnp.tile(p['diff_subln'][l], N_HEADS)[None],
        lam_init=0.8 - 0.6 * math.exp(-0.3 * l),
        w_out=p['w_out'][l].astype(BF16),
        ln1=jnp.stack([p['ln1_g'][l], p['ln1_b'][l]]),
        w_xq=p['w_xq'][l].astype(BF16),
        w_xo=p['w_xo'][l].astype(BF16),
        ln2=jnp.stack([p['ln2_g'][l], p['ln2_b'][l]]),
        w_router=w_router, b_router=b_router,
        w_gate=p['w_gate'][l].astype(BF16), w_up=p['w_up'][l].astype(BF16), w_down=p['w_down'][l].astype(BF16),
        ln3=jnp.stack([p['ln3_g'][l], p['ln3_b'][l]]),
        w_xkv_t=p['w_xkv'][l].astype(BF16).T,
    )


def _pack_state(conv_buf, h0):
    b, _, w = conv_buf.shape
    pad = jnp.zeros((b, SUBLANES - CONV_WIDTH, w), F32)
    return jnp.concatenate([conv_buf, h0[:, None], pad], axis=1)


def _finish_layer(x, ys, mem_kv_t, layer, lp, alpha, tm):
    b, t, d = x.shape
    x2 = _post(ys, x, lp['w_out'], lp['ln1'], lp['w_xq'], mem_kv_t, layer, lp['w_xo'], lp['ln2'], alpha, tm)
    x3 = _moe(x2.reshape(b * t, d), lp['w_router'], lp['b_router'], lp['w_gate'], lp['w_up'], lp['w_down'],
              lp['ln3'], alpha, 512)
    return x3.reshape(b, t, d)


def _token_major(x_t, lead):
    n_slots = x_t.shape[-2] // GROUP_WIDTH
    x = x_t.reshape(x_t.shape[:lead] + (n_slots, N_HEADS, HEAD_DIM, x_t.shape[-1]))
    return jnp.moveaxis(x, -1, lead)


def kernel(x_prompt, x_sample, cache_kv, cache_mem_kv, state_rglru_h, state_conv, page_table, mem_prompt, rel_bias, w_in, conv_w, conv_b, lru_wa, lru_ba, lru_wx, lru_bx, lru_lambda, diff_lambda, diff_subln, w_out, ln1_g, ln1_b, w_xq, w_xkv, w_xo, ln2_g, ln2_b, w_rg, b_rg, w_re, b_re, w_gate, w_up, w_down, ln3_g, ln3_b):
    p = dict(w_in=w_in, conv_w=conv_w, conv_b=conv_b, lru_wa=lru_wa, lru_ba=lru_ba, lru_wx=lru_wx, lru_bx=lru_bx,
             lru_lambda=lru_lambda, diff_lambda=diff_lambda, diff_subln=diff_subln, w_out=w_out, ln1_g=ln1_g,
             ln1_b=ln1_b, w_xq=w_xq, w_xkv=w_xkv, w_xo=w_xo, ln2_g=ln2_g, ln2_b=ln2_b, w_rg=w_rg, b_rg=b_rg,
             w_re=w_re, b_re=b_re, w_gate=w_gate, w_up=w_up, w_down=w_down, ln3_g=ln3_g, ln3_b=ln3_b)
    depth = w_in.shape[0]
    alpha = (2 * depth) ** 0.25
    bp, tp, d = x_prompt.shape
    bs, ts, _ = x_sample.shape
    w = GROUP_WIDTH
    n_pool, _, page = cache_kv.shape[:3]
    assert tp % TQ == 0 and ts <= SAMPLE_ROWS and (page_table.shape[1] * page) % MOBA_BLOCK == 0
    m_len = mem_prompt.shape[1]
    cache_t = jnp.moveaxis(cache_kv, 2, -1).reshape(n_pool, depth, N_KV_SLOTS * w, page)
    mem_sample_t = jnp.moveaxis(cache_mem_kv, 2, -1).reshape(bs, depth, 2 * w, m_len)

    tab_moba, tab_diff = rel_bias[:, :N_HEADS], rel_bias[:, N_HEADS:]
    pb_moba, far_moba = _prompt_bias(tab_moba)
    pb_diff, far_diff = _prompt_bias(tab_diff)
    sb_moba = _sample_bias(tab_moba, page, ts)
    sb_diff = jnp.concatenate([_sample_bias(tab_diff, page, ts)] * 2, axis=1)

    xp = x_prompt
    xs = jnp.pad(x_sample, ((0, 0), (0, SAMPLE_ROWS - ts), (0, 0)))
    st_p = jnp.zeros((bp, SUBLANES, w), F32)
    outs = dict(kv_p=[], mem_p=[], h_p=[], conv_p=[], kv_s=[], h_s=[], conv_s=[])
    for l in range(depth):
        lp = _layer_params(l, p)
        mkv_t = _memory_kv_t(mem_prompt, lp['w_xkv_t'])

        qag, kvt, kvb = _project(xp, lp['w_q'], lp['w_kv_t'])
        y_a, st = _rglru(qag, st_p, lp['pvec'], lp['w_gates'], tc=TQ, tv=TQ)
        y_b, y_c, y_d = _prompt_mixers(qag, kvt, kvb, pb_moba, far_moba, pb_diff, far_diff,
                                       lp['dl'], lp['subln'], lp['lam_init'])
        xp = _finish_layer(xp, (y_a, y_b, y_c, y_d), mkv_t, 0, lp, alpha, 512)
        outs['kv_p'].append(kvt)
        outs['mem_p'].append(mkv_t[:, 0])
        outs['h_p'].append(st[:, CONV_WIDTH - 1])
        outs['conv_p'].append(st[:, :CONV_WIDTH - 1])

        qag, kv = _matmul(xs.reshape(bs * SAMPLE_ROWS, d), [lp['w_q'], lp['w_kv']])
        qag, kv = qag.reshape(bs, SAMPLE_ROWS, 5 * w), kv.reshape(bs, SAMPLE_ROWS, N_KV_SLOTS * w)
        y_a, st = _rglru(qag, _pack_state(state_conv[:, l], state_rglru_h[:, l]), lp['pvec'], lp['w_gates'],
                         tc=SAMPLE_ROWS, tv=ts)
        new_page = jnp.pad(jnp.swapaxes(kv, 1, 2), ((0, 0), (0, 0), (0, page - SAMPLE_ROWS)))
        y3 = _paged_attention(qag[:, :, 2 * w:], new_page, cache_t, page_table, l, sb_moba, sb_diff,
                              lp['dl'], lp['subln'], lp['lam_init'])
        xs = _finish_layer(xs, (y_a, y3[:, :, :w], y3[:, :, w:2 * w], y3[:, :, 2 * w:]), mem_sample_t, l,
                           lp, alpha, SAMPLE_ROWS)
        outs['kv_s'].append(kv[:, :ts].reshape(bs, ts, N_KV_SLOTS, N_HEADS, HEAD_DIM))
        outs['h_s'].append(st[:, CONV_WIDTH - 1])
        outs['conv_s'].append(st[:, :CONV_WIDTH - 1])

    stack = lambda k: jnp.stack(outs[k], axis=1)
    return (xp, xs[:, :ts], _token_major(stack('kv_p'), 2), _token_major(stack('mem_p'), 2), stack('h_p'),
            stack('conv_p'), stack('kv_s'), stack('h_s'), stack('conv_s'))
```

```python
import functools
import math

import numpy as np
import jax
import jax.numpy as jnp
from jax import lax
from jax.experimental import pallas as pl
from jax.experimental.pallas import tpu as pltpu

F32 = jnp.float32
BF16 = jnp.bfloat16

HEAD_DIM = 64
N_HEADS = 4
GROUP_WIDTH = N_HEADS * HEAD_DIM
N_KV_SLOTS = 6
CONV_WIDTH = 4
LRU_C = 8.0
MOBA_BLOCK = 256
MOBA_TOPK = 3
DIFF_HALF = HEAD_DIM // 2
N_BUCKETS = 32
MAX_EXACT = N_BUCKETS // 2
MAX_DISTANCE = 128
N_GROUPS = 4
EXPERTS_PER_GROUP = 4
N_EXPERTS = N_GROUPS * EXPERTS_PER_GROUP
LN_EPS = 1e-5
NEG = -1e30
LOG2E = math.log2(math.e)
SOFTPLUS_LINEAR = 64.0
STICK_DEAD = -192.0
LANES = 128
SUBLANES = 8
VMEM_LIMIT = 56 * 1024 * 1024
TQ = MOBA_BLOCK
Q_TILES = 2
ROW_TILE = 1024
SAMPLE_ROWS = SUBLANES
PAGES_PER_STEP = 8


def _cparams(*sem):
    return pltpu.CompilerParams(dimension_semantics=sem, vmem_limit_bytes=VMEM_LIMIT)


def _softplus(x):
    return jnp.maximum(x, 0.0) + jnp.log1p(jnp.exp(-jnp.abs(x)))


def _sigmoid(x):
    return 1.0 / (1.0 + jnp.exp(-x))


def _dot(a, b, precision=None):
    return jnp.dot(a, b, preferred_element_type=F32, precision=precision)


def _dot_nt(a, b):
    return lax.dot_general(a, b, (((1,), (1,)), ((), ())), preferred_element_type=F32)


def _layer_norm(x, g, b):
    mu = jnp.mean(x, axis=-1, keepdims=True)
    xc = x - mu
    var = jnp.mean(xc * xc, axis=-1, keepdims=True)
    return xc * lax.rsqrt(var + LN_EPS) * g + b


def _bucket_table(n):
    d = np.arange(n)
    large = MAX_EXACT + (np.log(np.maximum(d, 1).astype(np.float32) / MAX_EXACT)
                         / math.log(MAX_DISTANCE / MAX_EXACT) * (N_BUCKETS - MAX_EXACT)).astype(np.int32)
    return np.where(d < MAX_EXACT, d, np.minimum(large, N_BUCKETS - 1)).astype(np.int32)


def _mm_kernel(*refs, n_out):
    x = refs[0][...].astype(BF16)
    for w_ref, o_ref in zip(refs[1:1 + n_out], refs[1 + n_out:]):
        o_ref[...] = _dot(x, w_ref[...])


def _matmul(x, ws, tm=512):
    m, k = x.shape
    tm = min(tm, m)
    return pl.pallas_call(
        functools.partial(_mm_kernel, n_out=len(ws)),
        grid=(m // tm,),
        in_specs=[pl.BlockSpec((tm, k), lambda i: (i, 0))] + [pl.BlockSpec(w.shape, lambda i: (0, 0)) for w in ws],
        out_specs=[pl.BlockSpec((tm, w.shape[1]), lambda i: (i, 0)) for w in ws],
        out_shape=[jax.ShapeDtypeStruct((m, w.shape[1]), F32) for w in ws],
        compiler_params=_cparams("parallel"),
    )(x, *ws)


def _proj_kernel(x_ref, wq_ref, wkvt_ref, q_ref, kvt_ref, kvb_ref):
    x = x_ref[0].astype(BF16)
    q_ref[0] = _dot(x, wq_ref[...])
    kvt = _dot_nt(wkvt_ref[...], x)
    kvt_ref[0] = kvt
    kvb_ref[0] = kvt.astype(BF16)


def _project(x, w_q, w_kv_t, tm=512):
    b, t, d = x.shape
    tm = min(tm, t)
    nq, nkv = w_q.shape[1], w_kv_t.shape[0]
    return pl.pallas_call(
        _proj_kernel,
        grid=(b, t // tm),
        in_specs=[pl.BlockSpec((1, tm, d), lambda i, j: (i, j, 0)),
                  pl.BlockSpec(w_q.shape, lambda i, j: (0, 0)),
                  pl.BlockSpec(w_kv_t.shape, lambda i, j: (0, 0))],
        out_specs=[pl.BlockSpec((1, tm, nq), lambda i, j: (i, j, 0)),
                   pl.BlockSpec((1, nkv, tm), lambda i, j: (i, 0, j)),
                   pl.BlockSpec((1, nkv, tm), lambda i, j: (i, 0, j))],
        out_shape=[jax.ShapeDtypeStruct((b, t, nq), F32), jax.ShapeDtypeStruct((b, nkv, t), F32),
                   jax.ShapeDtypeStruct((b, nkv, t), BF16)],
        compiler_params=_cparams("parallel", "parallel"),
    )(x, w_q, w_kv_t)


def _mem_kernel(m_ref, w_ref, o_ref):
    o_ref[0, 0] = _dot_nt(w_ref[...], m_ref[0].astype(BF16))


def _memory_kv_t(mem, w_xkv_t):
    b, m, d = mem.shape
    n = w_xkv_t.shape[0]
    return pl.pallas_call(
        _mem_kernel, grid=(b,),
        in_specs=[pl.BlockSpec((1, m, d), lambda i: (i, 0, 0)), pl.BlockSpec(w_xkv_t.shape, lambda i: (0, 0))],
        out_specs=pl.BlockSpec((1, 1, n, m), lambda i: (i, 0, 0, 0)),
        out_shape=jax.ShapeDtypeStruct((b, 1, n, m), F32),
        compiler_params=_cparams("parallel"),
    )(mem, w_xkv_t)


def _rglru_kernel(x_ref, g_ref, st_ref, pv_ref, w_ref, y_ref, so_ref, carry_ref, *, tc, tv):
    c = pl.program_id(1)

    @pl.when(c == 0)
    def _():
        carry_ref[...] = st_ref[0]

    w = GROUP_WIDTH
    x = x_ref[0]
    row = lax.broadcasted_iota(jnp.int32, (tc, w), 0)
    pv = pv_ref[...]
    conv_b, ba, bx, lam = pv[0:1], pv[1:2], pv[2:3], pv[3:4]
    prev = carry_ref[...]
    u = conv_b + pv[4 + CONV_WIDTH - 1:4 + CONV_WIDTH] * x
    for sft in range(1, CONV_WIDTH):
        xs = pltpu.roll(x, sft, 0)
        for i in range(sft):
            xs = jnp.where(row == i, prev[CONV_WIDTH - 1 + i - sft:CONV_WIDTH + i - sft], xs)
        u = u + pv[4 + CONV_WIDTH - 1 - sft:4 + CONV_WIDTH - sft] * xs
    gates = _dot(u.astype(BF16), w_ref[...])
    r = _sigmoid(gates[:, :w] + ba)
    ig = _sigmoid(gates[:, w:] + bx)
    log_a = -LRU_C * r * _softplus(-lam)
    a = jnp.exp(log_a)
    b = jnp.sqrt(-jnp.tanh(log_a) * (jnp.exp(2.0 * log_a) + 1.0)) * ig * u
    s = 1
    while s < tc:
        a_s = pltpu.roll(a, s, 0)
        b_s = pltpu.roll(b, s, 0)
        m = row >= s
        b = jnp.where(m, a * b_s + b, b)
        a = jnp.where(m, a * a_s, a)
        s *= 2
    h = b + a * prev[CONV_WIDTH - 1:CONV_WIDTH]
    gt = g_ref[0]
    gelu = 0.5 * gt * (1.0 + jnp.tanh(math.sqrt(2.0 / math.pi) * (gt + 0.044715 * (gt * gt * gt))))
    y_ref[0] = h * gelu
    carry_ref[0:CONV_WIDTH - 1] = x[tv - (CONV_WIDTH - 1):tv]
    carry_ref[CONV_WIDTH - 1:CONV_WIDTH] = h[tv - 1:tv]

    @pl.when(c == pl.num_programs(1) - 1)
    def _():
        so_ref[0] = carry_ref[...]


def _rglru(qag, state, pvec, w_gates, tc, tv):
    b, t, _ = qag.shape
    w = GROUP_WIDTH
    assert t % tc == 0 and tv >= CONV_WIDTH - 1
    return pl.pallas_call(
        functools.partial(_rglru_kernel, tc=tc, tv=tv),
        grid=(b, t // tc),
        in_specs=[pl.BlockSpec((1, tc, w), lambda i, c: (i, c, 0)),
                  pl.BlockSpec((1, tc, w), lambda i, c: (i, c, 1)),
                  pl.BlockSpec((1, SUBLANES, w), lambda i, c: (i, 0, 0)),
                  pl.BlockSpec((SUBLANES, w), lambda i, c: (0, 0)),
                  pl.BlockSpec((w, 2 * w), lambda i, c: (0, 0))],
        out_specs=[pl.BlockSpec((1, tc, w), lambda i, c: (i, c, 0)),
                   pl.BlockSpec((1, SUBLANES, w), lambda i, c: (i, 0, 0))],
        out_shape=[jax.ShapeDtypeStruct((b, t, w), F32), jax.ShapeDtypeStruct((b, SUBLANES, w), F32)],
        scratch_shapes=[pltpu.VMEM((SUBLANES, w), F32)],
        compiler_params=_cparams("parallel", "arbitrary"),
    )(qag, qag, state, pvec, w_gates)


def _tri_tables(n_super):
    qs, ks = [], []
    for qs_i in range(n_super):
        for ki in range(Q_TILES * qs_i + Q_TILES - 1, -1, -1):
            qs.append(qs_i)
            ks.append(ki)
    return jnp.asarray(np.array(qs, np.int32)), jnp.asarray(np.array(ks, np.int32))


def _wide(x):
    return jnp.concatenate([x, x], axis=1)


def _values_with(vt, h, fill):
    rowh = lax.broadcasted_iota(jnp.int32, vt.shape, 0) // HEAD_DIM
    return jnp.where(rowh == h, vt, jnp.full((), fill, vt.dtype))


def _softmax_update(j, rows, t2, m_cand, vt_aug, m_ref, acc_ref, shift=None):
    m_old = m_ref[j, rows, :]
    m_new = jnp.maximum(m_old, m_cand)
    off = m_new if shift is None else m_new - shift
    p = jnp.exp2(t2 - _wide(off))
    acc_ref[j, rows, :] = jnp.exp2(m_old - m_new) * acc_ref[j, rows, :] + _dot_nt(p.astype(BF16), vt_aug)
    m_ref[j, rows, :] = m_new


def _normalised(acc):
    return acc / pltpu.roll(acc, HEAD_DIM, 1)


def _moba_select(q, km, q_blk):
    blk = lax.broadcasted_iota(jnp.int32, q_blk.shape, 1)
    blkf = blk.astype(F32)
    gate = _dot(q, km, precision=lax.Precision.HIGHEST)
    g = jnp.where(blk < q_blk, gate, -jnp.inf)
    sel = jnp.zeros(q_blk.shape, F32)
    for _ in range(MOBA_TOPK):
        mx = jnp.max(g, axis=1, keepdims=True)
        idx = jnp.min(jnp.where(g == mx, blkf, float(LANES)), axis=1, keepdims=True)
        hit = blkf == idx
        sel = jnp.where(hit, 1.0, sel)
        g = jnp.where(hit, -jnp.inf, g)
    return jnp.where(blk < q_blk, sel, 0.0)


def _diff_lambda(dl, lam_init):
    return (jnp.exp(jnp.sum(dl[0:1] * dl[1:2], axis=1, keepdims=True))
            - jnp.exp(jnp.sum(dl[2:3] * dl[3:4], axis=1, keepdims=True)) + lam_init)


def _mix_kernel(qt_ref, kt_ref, qb_ref, qc_ref, qd_ref, kv_ref,
                ntri_ref, km_ref, bias_b_ref, far_b_ref, bias_d_ref, far_d_ref, dl_ref, sub_ref,
                ob_ref, oc_ref, od_ref,
                qmb_ref, sel_ref, mb_ref, accb_ref, qmc_ref, run_ref, accc_ref, qmd_ref, md_ref, accd_ref,
                live_ref, *, lam_init):
    t = pl.program_id(2)
    qs, ki = qt_ref[t], kt_ref[t]
    off = ki - Q_TILES * qs

    @pl.when(off == Q_TILES - 1)
    def _():
        qb = qb_ref[0]
        qb2 = qb * (HEAD_DIM ** -0.5 * LOG2E)
        qc2 = qc_ref[0] * (HEAD_DIM ** -0.5 * LOG2E)
        qd2 = qd_ref[0] * (DIFF_HALF ** -0.5 * LOG2E)
        q_blk = Q_TILES * qs + lax.broadcasted_iota(jnp.int32, (Q_TILES * TQ, LANES), 0) // TQ
        for h in range(2):
            hs = slice(h * HEAD_DIM, (h + 1) * HEAD_DIM)
            sel_ref[:, h * LANES:(h + 1) * LANES] = _moba_select(qb[:, hs], km_ref[0, hs, :], q_blk).astype(BF16)
            qmb_ref[h] = qb2[:, hs].astype(BF16)
            qmc_ref[h] = qc2[:, hs].astype(BF16)
        for j in range(4):
            qmd_ref[j] = qd2[:, j * DIFF_HALF:(j + 1) * DIFF_HALF].astype(BF16)
        mb_ref[...] = jnp.full_like(mb_ref, NEG)
        md_ref[...] = jnp.full_like(md_ref, NEG)
        accb_ref[...] = jnp.zeros_like(accb_ref)
        accc_ref[...] = jnp.zeros_like(accc_ref)
        accd_ref[...] = jnp.zeros_like(accd_ref)
        run_ref[...] = jnp.zeros_like(run_ref)
        live_ref[0] = 1


    def moba_step(rows, dist):
        kt = kv_ref[0, 0, 0]
        vt = kv_ref[0, 1, 0]
        if dist != 0:
            hit = lax.broadcasted_iota(jnp.int32, (LANES, LANES), 0) == ki
            e = jnp.where(hit, 1.0, 0.0).astype(BF16)
            z = jnp.zeros_like(e)
            onehot = jnp.concatenate([jnp.concatenate([e, z], axis=1), jnp.concatenate([z, e], axis=1)], axis=0)
            picked = _dot(sel_ref[rows, :], onehot)
        for h in range(2):
            raw = _dot(qmb_ref[h, rows, :], kt[h * HEAD_DIM:(h + 1) * HEAD_DIM])
            vt_aug = _values_with(vt, h, 1.0)
            row_pen = None if dist == 0 else jnp.where(picked[:, h * LANES:(h + 1) * LANES] > 0.5, 0.0, NEG)
            if dist is not None:
                t2 = raw + bias_b_ref[h, dist]
                m_cand = jnp.max(t2, axis=1, keepdims=True)
                m_cand = m_cand if dist == 0 else m_cand + row_pen
                _softmax_update(h, rows, t2, m_cand, vt_aug, mb_ref, accb_ref, shift=row_pen)
            else:
                shift = far_b_ref[h, 0:1] + row_pen
                m_cand = jnp.max(raw, axis=1, keepdims=True) + shift
                _softmax_update(h, rows, raw, m_cand, vt_aug, mb_ref, accb_ref, shift=shift)

    def stick_step(rows, dist):
        kt = kv_ref[0, 2, 0]
        vt = kv_ref[0, 3, 0]
        n = rows.stop - rows.start
        if dist == 0:
            r = lax.broadcasted_iota(jnp.int32, (TQ, TQ), 0)
            c = lax.broadcasted_iota(jnp.int32, (TQ, TQ), 1)
            valid = c < r
        z2s, runs, parts = [], [], []
        for h in range(2):
            z2 = _dot(qmc_ref[h, rows, :], kt[h * HEAD_DIM:(h + 1) * HEAD_DIM])
            sp2 = jnp.where(z2 > SOFTPLUS_LINEAR, z2, jnp.log2(1.0 + jnp.exp2(z2)))
            if dist == 0:
                sp2 = jnp.where(valid, sp2, 0.0)
            z2s.append(z2)
            parts.append(sp2.astype(BF16))
            runs.append(run_ref[h, rows, :] - jnp.broadcast_to(jnp.sum(sp2, axis=1, keepdims=True), (n, LANES)))
        cums = _dot(jnp.concatenate(parts, axis=0), ntri_ref[...])
        acc = accc_ref[rows, :]
        for h in range(2):
            e = jnp.exp2(z2s[h] + cums[h * n:(h + 1) * n] + _wide(run_ref[h, rows, :]))
            if dist == 0:
                e = jnp.where(valid, e, 0.0)
            acc = acc + _dot_nt(e.astype(BF16), _values_with(vt, h, 0.0))
            run_ref[h, rows, :] = runs[h]
        accc_ref[rows, :] = acc

    def diff_step(rows, dist):
        kt = kv_ref[0, 4, 0]
        vt = kv_ref[0, 5, 0]
        vt_aug = [_values_with(vt, h, 1.0) for h in range(2)]
        for j in range(4):
            h = j // 2
            raw = _dot(qmd_ref[j, rows, :], kt[j * DIFF_HALF:(j + 1) * DIFF_HALF])
            if dist is not None:
                t2 = raw + bias_d_ref[h, dist]
                _softmax_update(j, rows, t2, jnp.max(t2, axis=1, keepdims=True), vt_aug[h], md_ref, accd_ref)
            else:
                shift = far_d_ref[h, 0:1]
                m_cand = jnp.max(raw, axis=1, keepdims=True) + shift
                _softmax_update(j, rows, raw, m_cand, vt_aug[h], md_ref, accd_ref, shift=shift)

    def all_mixers(rows, dist):
        moba_step(rows, dist)
        stick_step(rows, dist)
        diff_step(rows, dist)

    def near_diagonal_step(o):
        for r in range(max(o, 0), min(o + 2, Q_TILES)):
            all_mixers(slice(r * TQ, (r + 1) * TQ), r - o)
        if o + 2 < Q_TILES:
            all_mixers(slice((o + 2) * TQ, Q_TILES * TQ), None)

    for o in range(Q_TILES - 1, -2, -1):
        pl.when(off == o)(functools.partial(near_diagonal_step, o))

    @pl.when(off <= -2)
    def _():
        rows = slice(0, Q_TILES * TQ)
        moba_step(rows, None)
        diff_step(rows, None)

        @pl.when(live_ref[0] == 1)
        def _():
            stick_step(rows, None)
            live_ref[0] = (jnp.max(run_ref[...]) >= STICK_DEAD).astype(jnp.int32)

    @pl.when(ki == 0)
    def _():
        head = lax.broadcasted_iota(jnp.int32, (Q_TILES * TQ, LANES), 1) // HEAD_DIM
        ob_ref[0] = jnp.where(head == 0, _normalised(accb_ref[0]), _normalised(accb_ref[1]))
        oc_ref[0] = accc_ref[...]
        lam = _diff_lambda(dl_ref[...], lam_init)
        o = jnp.where(head == 0,
                      _normalised(accd_ref[0]) - lam * _normalised(accd_ref[1]),
                      _normalised(accd_ref[2]) - lam * _normalised(accd_ref[3]))
        o2 = o * o
        ms = jnp.where(head == 0,
                       jnp.sum(jnp.where(head == 0, o2, 0.0), axis=1, keepdims=True),
                       jnp.sum(jnp.where(head == 1, o2, 0.0), axis=1, keepdims=True)) * (1.0 / HEAD_DIM)
        od_ref[0] = o * lax.rsqrt(ms + LN_EPS) * sub_ref[...] * (1.0 - lam_init)


def _kmean_kernel(k_ref, ones_ref, o_ref):
    o_ref[0] = _dot(k_ref[0], ones_ref[...], precision=lax.Precision.HIGHEST)


def _suffix_tri(n, sign=1.0):
    i = np.arange(n)
    return jnp.asarray(sign * (i[:, None] >= i[None, :]).astype(np.float32), dtype=BF16)


def _prompt_mixers(qag, kvt, kvb, bias_b, far_b, bias_d, far_d, dl, subln, lam_init):
    b, t, _ = qag.shape
    kv5 = kvb.reshape(b, N_KV_SLOTS, 2, LANES, t)
    nb = t // MOBA_BLOCK
    assert nb <= LANES
    blk_mean = np.zeros((t, LANES), np.float32)
    blk_mean[np.arange(t), np.arange(t) // MOBA_BLOCK] = 1.0 / MOBA_BLOCK
    kmean_t = pl.pallas_call(
        _kmean_kernel, grid=(b,),
        in_specs=[pl.BlockSpec((1, GROUP_WIDTH, t), lambda i: (i, 0, 0)),
                  pl.BlockSpec((t, LANES), lambda i: (0, 0))],
        out_specs=pl.BlockSpec((1, GROUP_WIDTH, LANES), lambda i: (i, 0, 0)),
        out_shape=jax.ShapeDtypeStruct((b, GROUP_WIDTH, LANES), F32),
        compiler_params=_cparams("parallel"),
    )(kvt, jnp.asarray(blk_mean))
    rows = Q_TILES * TQ
    assert t % rows == 0
    qt, kt = _tri_tables(t // rows)
    q_spec = lambda col: pl.BlockSpec((1, rows, LANES), lambda b, p, t, qt, kt: (b, qt[t], col + p))
    kv_spec = pl.BlockSpec((1, N_KV_SLOTS, 1, LANES, TQ), lambda b, p, t, qt, kt: (b, 0, p, 0, kt[t]))
    bias_spec = pl.BlockSpec((2, 2, TQ, TQ), lambda b, p, t, qt, kt: (p, 0, 0, 0))
    far_spec = pl.BlockSpec((2, SUBLANES, LANES), lambda b, p, t, qt, kt: (p, 0, 0))
    const2 = lambda shape: pl.BlockSpec(shape, lambda b, p, t, qt, kt: (0, 0))
    out_spec = pl.BlockSpec((1, rows, LANES), lambda b, p, t, qt, kt: (b, qt[t], p))
    out = jax.ShapeDtypeStruct((b, t, GROUP_WIDTH), F32)
    stats = lambda n: pltpu.VMEM((n, rows, LANES), F32)
    return pl.pallas_call(
        functools.partial(_mix_kernel, lam_init=lam_init),
        grid_spec=pltpu.PrefetchScalarGridSpec(
            num_scalar_prefetch=2, grid=(b, 2, qt.shape[0]),
            in_specs=[q_spec(4), q_spec(6), q_spec(8), kv_spec]
                     + [const2((TQ, TQ)), pl.BlockSpec((1, LANES, LANES), lambda b, p, t, qt, kt: (b, p, 0)),
                        bias_spec, far_spec, bias_spec, far_spec, const2(dl.shape), const2((1, LANES))],
            out_specs=[out_spec] * 3,
            scratch_shapes=[pltpu.VMEM((2, rows, HEAD_DIM), BF16), pltpu.VMEM((rows, 2 * LANES), BF16),
                            stats(2), stats(2),
                            pltpu.VMEM((2, rows, HEAD_DIM), BF16), stats(2), pltpu.VMEM((rows, LANES), F32),
                            pltpu.VMEM((4, rows, DIFF_HALF), BF16), stats(4), stats(4),
                            pltpu.SMEM((1,), jnp.int32)]),
        out_shape=[out] * 3,
        compiler_params=_cparams("parallel", "parallel", "arbitrary"),
    )(qt, kt, qag, qag, qag, kv5, _suffix_tri(TQ, -1.0), kmean_t,
      bias_b, far_b, bias_d, far_d, dl, subln[:, :LANES])


def _toeplitz(g):
    h, l = g.shape
    n = l // 2
    w = jnp.concatenate([g[:, :1], jnp.flip(g[:, 1:], axis=1)], axis=1)
    rep = jnp.tile(w, (1, n))[:, :n * (l - 1)].reshape(h, n, l - 1)
    return rep[:, :, :n]


def _prompt_bias(tab):
    n = TQ
    bucket = _bucket_table(4 * n)
    assert (bucket[n + 1:] == bucket[-1]).all()
    x = np.arange(2 * n)
    d = np.where(x < n, x, x - 2 * n)
    fvec = (tab * LOG2E).T
    g0 = jnp.where(jnp.asarray(d >= 0), fvec[:, bucket[np.maximum(d, 0)]], NEG)
    g1 = fvec[:, bucket[n + d]]
    tiles = jnp.stack([_toeplitz(g0), _toeplitz(g1)], axis=1)
    far = jnp.broadcast_to(fvec[:, bucket[-1]][:, None, None], (tab.shape[1], SUBLANES, LANES))
    return tiles, far


def _expand_rows(x4):
    x = jnp.concatenate([x4] * N_HEADS, axis=0)
    rh = lax.broadcasted_iota(jnp.int32, x.shape, 0) // SAMPLE_ROWS
    lh = lax.broadcasted_iota(jnp.int32, x.shape, 1) // HEAD_DIM
    return jnp.where(rh == lh, x, 0.0)


def _collapse_rows(x):
    rh = lax.broadcasted_iota(jnp.int32, x.shape, 0) // SAMPLE_ROWS
    lh = lax.broadcasted_iota(jnp.int32, x.shape, 1) // HEAD_DIM
    x = jnp.where(rh == lh, x, 0.0)
    out = x[0:SAMPLE_ROWS]
    for h in range(1, N_HEADS):
        out = out + x[h * SAMPLE_ROWS:(h + 1) * SAMPLE_ROWS]
    return out


def _paged_kernel(pt_ref, q_ref, new_ref, *rest, g_pages, lam_init):
    page_refs = rest[:g_pages]
    bm_ref, bd_ref, tri_ref, dl_ref, sub_ref, o_ref = rest[g_pages:g_pages + 6]
    (qb_ref, qc_ref, qd_ref, mb_m, mb_l, mb_g, mb_acc,
     run_ref, accc_ref, md_ref, ld_ref, accd_ref) = rest[g_pages + 6:]
    s_idx = pl.program_id(1)
    n_steps = pl.num_programs(1)
    nblk = mb_m.shape[0] - 1
    w = GROUP_WIDTH
    rows = N_HEADS * SAMPLE_ROWS
    page = new_ref.shape[2]
    qq = lax.broadcasted_iota(jnp.int32, (rows, page), 0) % SAMPLE_ROWS
    kk = lax.broadcasted_iota(jnp.int32, (rows, page), 1)
    tri = tri_ref[...]

    def slot(pg, i):
        return pg[i * w:(i + 1) * w, :].astype(BF16)

    def moba_block(pages, biases, pens, n):
        ss, gsum = [], None
        for pg, bias, pen in zip(pages, biases, pens):
            raw = _dot(qb_ref[...], slot(pg, 0))
            sc = raw + bias
            ss.append(sc if pen is None else sc + pen)
            gsum = raw if gsum is None else gsum + raw
        m = ss[0].max(axis=1, keepdims=True)
        for sc in ss[1:]:
            m = jnp.maximum(m, sc.max(axis=1, keepdims=True))
        psum, acc = None, None
        for pg, sc in zip(pages, ss):
            p = jnp.exp(sc - m)
            pa = _dot_nt(p.astype(BF16), slot(pg, 1))
            psum = p if psum is None else psum + p
            acc = pa if acc is None else acc + pa
        mb_m[n] = m
        mb_l[n] = jnp.sum(psum, axis=1, keepdims=True)
        mb_acc[n] = acc
        mb_g[n] = jnp.sum(gsum, axis=1, keepdims=True)

    def stick_page(pg, valid):
        z = _dot(qc_ref[...], slot(pg, 2))
        lk = -_softplus(z)
        if valid is not None:
            lk = jnp.where(valid, lk, 0.0)
        hi = lk.astype(BF16)
        lo = (lk - hi.astype(F32)).astype(BF16)
        cum = _dot(hi, tri) + _dot(lo, tri)
        e = jnp.exp(z + cum + run_ref[...])
        if valid is not None:
            e = jnp.where(valid, e, 0.0)
        accc_ref[...] += _dot_nt(e.astype(BF16), slot(pg, 3))
        run_ref[...] += jnp.sum(lk, axis=1, keepdims=True)

    def diff_pages(pages, biases, pens):
        ss = []
        for pg, bias, pen in zip(pages, biases, pens):
            sc = _dot(qd_ref[...], slot(pg, 4)) * DIFF_HALF ** -0.5 + bias
            ss.append(sc if pen is None else sc + pen)
        mx = ss[0]
        for sc in ss[1:]:
            mx = jnp.maximum(mx, sc)
        m_old = md_ref[...]
        m_new = jnp.maximum(m_old, mx.max(axis=1, keepdims=True))
        alpha = jnp.exp(m_old - m_new)
        psum, acc = None, None
        for pg, sc in zip(pages, ss):
            p = jnp.exp(sc - m_new)
            pa = _dot_nt(p.astype(BF16), slot(pg, 5))
            psum = p if psum is None else psum + p
            acc = pa if acc is None else acc + pa
        ld_ref[...] = alpha * ld_ref[...] + jnp.sum(psum, axis=1, keepdims=True)
        accd_ref[...] = alpha * accd_ref[...] + acc
        md_ref[...] = m_new

    @pl.when(s_idx == 0)
    def _():
        q = q_ref[0]
        qb_ref[...] = (_expand_rows(q[:, 0:w]) * HEAD_DIM ** -0.5).astype(BF16)
        qc_ref[...] = (_expand_rows(q[:, w:2 * w]) * HEAD_DIM ** -0.5).astype(BF16)
        qdx = _expand_rows(q[:, 2 * w:3 * w])
        half = lax.broadcasted_iota(jnp.int32, qdx.shape, 1) // DIFF_HALF % 2
        qd_ref[0:rows] = jnp.where(half == 0, qdx, 0.0).astype(BF16)
        qd_ref[rows:2 * rows] = jnp.where(half == 1, qdx, 0.0).astype(BF16)
        run_ref[...] = jnp.zeros_like(run_ref)
        accc_ref[...] = jnp.zeros_like(accc_ref)
        md_ref[...] = jnp.full_like(md_ref, NEG)
        ld_ref[...] = jnp.zeros_like(ld_ref)
        accd_ref[...] = jnp.zeros_like(accd_ref)
        new = new_ref[0]
        causal_pen = jnp.where(kk <= qq, 0.0, NEG)
        moba_block([new], [bm_ref[0]], [causal_pen], nblk)
        stick_page(new, kk < qq)
        diff_pages([new], [bd_ref[0]], [jnp.concatenate([causal_pen, causal_pen], axis=0)])

    pages = [r[0, 0] for r in page_refs]
    bm0 = jnp.where(s_idx == 0, bm_ref[1], bm_ref[2])
    bd0 = jnp.where(s_idx == 0, bd_ref[1], bd_ref[2])
    bms = [bm0] + [bm_ref[2]] * (g_pages - 1)
    bds = [bd0] + [bd_ref[2]] * (g_pages - 1)
    per_block = MOBA_BLOCK // page
    for i in range(0, g_pages, per_block):
        n = nblk - 1 - (s_idx * g_pages + i) // per_block
        moba_block(pages[i:i + per_block], bms[i:i + per_block], [None] * per_block, n)
    for pg in pages:
        stick_page(pg, None)
    diff_pages(pages, bds, [None] * g_pages)

    @pl.when(s_idx == n_steps - 1)
    def _():
        blkf = lax.broadcasted_iota(jnp.int32, (nblk, rows, 1), 0).astype(F32)
        g = mb_g[0:nblk]
        sel = jnp.zeros((nblk, rows, 1), F32)
        for _ in range(min(MOBA_TOPK, nblk)):
            mx = jnp.max(g, axis=0, keepdims=True)
            idx = jnp.min(jnp.where(g == mx, blkf, float(nblk)), axis=0, keepdims=True)
            hit = blkf == idx
            sel = jnp.where(hit, 1.0, sel)
            g = jnp.where(hit, -jnp.inf, g)
        m_own = mb_m[nblk]
        m_tot = jnp.maximum(m_own, jnp.max(jnp.where(sel > 0.5, mb_m[0:nblk], NEG), axis=0))
        wgt = jnp.where(sel > 0.5, jnp.exp(mb_m[0:nblk] - m_tot), 0.0)
        w_own = jnp.exp(m_own - m_tot)
        l_tot = w_own * mb_l[nblk] + jnp.sum(wgt * mb_l[0:nblk], axis=0)
        a_tot = w_own * mb_acc[nblk] + jnp.sum(wgt * mb_acc[0:nblk], axis=0)
        o_ref[0, :, 0:w] = _collapse_rows(a_tot / l_tot)
        o_ref[0, :, w:2 * w] = _collapse_rows(accc_ref[...])
        lam = _diff_lambda(dl_ref[...], lam_init)
        od = (accd_ref[0:rows] / ld_ref[0:rows] - lam * (accd_ref[rows:2 * rows] / ld_ref[rows:2 * rows]))
        od = _collapse_rows(od)
        od2 = od * od
        lh = lax.broadcasted_iota(jnp.int32, od.shape, 1) // HEAD_DIM
        ms = jnp.zeros_like(od)
        for h in range(N_HEADS):
            ms = jnp.where(lh == h, jnp.sum(jnp.where(lh == h, od2, 0.0), axis=1, keepdims=True), ms)
        o_ref[0, :, 2 * w:3 * w] = od * lax.rsqrt(ms * (1.0 / HEAD_DIM) + LN_EPS) * sub_ref[...] * (1.0 - lam_init)


def _paged_attention(q3, new_page, cache_t, page_table, layer, bm, bd, dl, subln, lam_init):
    b = q3.shape[0]
    n_pages = page_table.shape[1]
    page = cache_t.shape[3]
    g_pages = min(PAGES_PER_STEP, n_pages)
    assert n_pages % g_pages == 0 and MOBA_BLOCK % page == 0 and g_pages % (MOBA_BLOCK // page) == 0
    nblk = n_pages * page // MOBA_BLOCK
    rows = N_HEADS * SAMPLE_ROWS
    w = GROUP_WIDTH

    def page_spec(i):
        return pl.BlockSpec((1, 1, N_KV_SLOTS * w, page),
                            lambda bi, s, pt: (pt[bi, n_pages - 1 - (s * g_pages + i)], layer, 0, 0))

    const2 = lambda bi, s, pt: (0, 0)
    const3 = lambda bi, s, pt: (0, 0, 0)
    return pl.pallas_call(
        functools.partial(_paged_kernel, g_pages=g_pages, lam_init=lam_init),
        grid_spec=pltpu.PrefetchScalarGridSpec(
            num_scalar_prefetch=1, grid=(b, n_pages // g_pages),
            in_specs=[pl.BlockSpec((1, SAMPLE_ROWS, 3 * w), lambda bi, s, pt: (bi, 0, 0)),
                      pl.BlockSpec((1, N_KV_SLOTS * w, page), lambda bi, s, pt: (bi, 0, 0))]
                     + [page_spec(i) for i in range(g_pages)]
                     + [pl.BlockSpec(bm.shape, const3), pl.BlockSpec(bd.shape, const3),
                        pl.BlockSpec((page, page), const2), pl.BlockSpec(dl.shape, const2),
                        pl.BlockSpec((1, w), const2)],
            out_specs=pl.BlockSpec((1, SAMPLE_ROWS, 3 * w), lambda bi, s, pt: (bi, 0, 0)),
            scratch_shapes=[pltpu.VMEM((rows, w), BF16), pltpu.VMEM((rows, w), BF16),
                            pltpu.VMEM((2 * rows, w), BF16),
                            pltpu.VMEM((nblk + 1, rows, 1), F32), pltpu.VMEM((nblk + 1, rows, 1), F32),
                            pltpu.VMEM((nblk + 1, rows, 1), F32), pltpu.VMEM((nblk + 1, rows, w), F32),
                            pltpu.VMEM((rows, 1), F32), pltpu.VMEM((rows, w), F32),
                            pltpu.VMEM((2 * rows, 1), F32), pltpu.VMEM((2 * rows, 1), F32),
                            pltpu.VMEM((2 * rows, w), F32)]),
        out_shape=jax.ShapeDtypeStruct((b, SAMPLE_ROWS, 3 * w), F32),
        compiler_params=_cparams("parallel", "arbitrary"),
    )(page_table, q3, new_page, *([cache_t] * g_pages), bm, bd, _suffix_tri(page), dl, subln)


def _sample_bias(tab, page, ts):
    bucket = _bucket_table(2 * page + SAMPLE_ROWS)
    assert (bucket[page + 1:] == bucket[-1]).all()
    q = np.minimum(np.arange(SAMPLE_ROWS), ts - 1)
    k = np.arange(page)
    idx = np.stack([bucket[np.maximum(q[:, None] - k[None, :], 0)],
                    bucket[page + q[:, None] - k[None, :]],
                    np.full((SAMPLE_ROWS, page), bucket[-1])])
    b = jnp.transpose(tab[jnp.asarray(idx)], (0, 3, 1, 2))
    return b.reshape(3, N_HEADS * SAMPLE_ROWS, page)


def _post_kernel(ya_ref, yb_ref, yc_ref, yd_ref, x_ref, wo_ref, ln1_ref, wq_ref, mem_ref, wxo_ref, ln2_ref,
                 o_ref, *, alpha):
    w = GROUP_WIDTH
    mixed = None
    for i, y_ref in enumerate((ya_ref, yb_ref, yc_ref, yd_ref)):
        part = _dot(y_ref[0].astype(BF16), wo_ref[i * w:(i + 1) * w, :])
        mixed = part if mixed is None else mixed + part
    x1 = _layer_norm(alpha * x_ref[0] + mixed, ln1_ref[0:1], ln1_ref[1:2])
    qx = _dot(x1.astype(BF16), wq_ref[...])
    kt = mem_ref[0, 0, :w, :].astype(BF16)
    vt = mem_ref[0, 0, w:, :].astype(BF16)
    head = lax.broadcasted_iota(jnp.int32, qx.shape, 1) // HEAD_DIM
    o = jnp.zeros_like(qx)
    for h in range(N_HEADS):
        s = _dot(jnp.where(head == h, qx, 0.0).astype(BF16), kt) * HEAD_DIM ** -0.5
        p = jnp.exp(s - jnp.max(s, axis=1, keepdims=True))
        p = p / jnp.sum(p, axis=1, keepdims=True)
        o = jnp.where(head == h, _dot_nt(p.astype(BF16), vt), o)
    xo = _dot(o.astype(BF16), wxo_ref[...])
    o_ref[0] = _layer_norm(alpha * x1 + xo, ln2_ref[0:1], ln2_ref[1:2])


def _post(ys, x, w_out, ln1, w_xq, mem_kv_t, layer, w_xo, ln2, alpha, tm):
    b, t, d = x.shape
    w = GROUP_WIDTH
    tm = min(tm, t)
    m_len = mem_kv_t.shape[3]
    row = lambda width: pl.BlockSpec((1, tm, width), lambda i, j: (i, j, 0))
    const = lambda shape: pl.BlockSpec(shape, lambda i, j: (0,) * len(shape))
    return pl.pallas_call(
        functools.partial(_post_kernel, alpha=alpha),
        grid=(b, t // tm),
        in_specs=[row(w)] * 4 + [row(d), const(w_out.shape), const(ln1.shape), const(w_xq.shape),
                                 pl.BlockSpec((1, 1, 2 * w, m_len), lambda i, j: (i, layer, 0, 0)),
                                 const(w_xo.shape), const(ln2.shape)],
        out_specs=row(d),
        out_shape=jax.ShapeDtypeStruct((b, t, d), F32),
        compiler_params=_cparams("parallel", "parallel"),
    )(*ys, x, w_out, ln1, w_xq, mem_kv_t, w_xo, ln2)


def _route(x, wr_ref, br_ref):
    logits = _dot(x, wr_ref[...], precision=lax.Precision.HIGHEST) + br_ref[...]
    lane = lax.broadcasted_iota(jnp.int32, logits.shape, 1)
    lanef = lane.astype(F32)
    is_grp = jnp.logical_and(lane >= N_EXPERTS, lane < N_EXPERTS + N_GROUPS)
    lg = jnp.where(is_grp, logits, -jnp.inf)
    gmax = jnp.max(lg, axis=1, keepdims=True)
    pg_sel = 1.0 / jnp.sum(jnp.exp(lg - gmax), axis=1, keepdims=True)
    gsel = jnp.min(jnp.where(lg == gmax, lanef, float(LANES)), axis=1, keepdims=True) - N_EXPERTS
    in_grp = jnp.logical_and(lane < N_EXPERTS, (lane // EXPERTS_PER_GROUP).astype(F32) == gsel)
    le = jnp.where(in_grp, logits, -jnp.inf)
    pe = jnp.exp(le - jnp.max(le, axis=1, keepdims=True))
    pe = pe / jnp.sum(pe, axis=1, keepdims=True)
    cand = jnp.where(in_grp, pe, -1.0)
    p1 = jnp.max(cand, axis=1, keepdims=True)
    i1 = jnp.min(jnp.where(cand == p1, lanef, float(LANES)), axis=1, keepdims=True)
    cand = jnp.where(lanef == i1, -1.0, cand)
    p2 = jnp.max(cand, axis=1, keepdims=True)
    i2 = jnp.min(jnp.where(cand == p2, lanef, float(LANES)), axis=1, keepdims=True)
    top = jnp.where(lanef == i1, p1, 0.0) + jnp.where(lanef == i2, p2, 0.0)
    return pg_sel * top / (p1 + p2)


def _moe_kernel(x_ref, wr_ref, br_ref, wg_ref, wu_ref, wd_ref, ln_ref, o_ref, xb_ref, cw_ref, acc_ref, *, alpha):
    e = pl.program_id(1)

    @pl.when(e == 0)
    def _():
        x = x_ref[...]
        xb_ref[...] = x.astype(BF16)
        cw_ref[...] = _route(x, wr_ref, br_ref)
        acc_ref[...] = jnp.zeros_like(acc_ref)

    xb = xb_ref[...]
    gate = _dot(xb, wg_ref[0])
    hid = gate * _sigmoid(gate) * _dot(xb, wu_ref[0])
    y = _dot(hid.astype(BF16), wd_ref[0])
    lane = lax.broadcasted_iota(jnp.int32, cw_ref.shape, 1)
    cw = jnp.sum(jnp.where(lane == e, cw_ref[...], 0.0), axis=1, keepdims=True)
    acc_ref[...] += cw * y

    @pl.when(e == pl.num_programs(1) - 1)
    def _():
        o_ref[...] = _layer_norm(alpha * x_ref[...] + acc_ref[...], ln_ref[0:1], ln_ref[1:2])


def _moe(x, w_router, b_router, w_gate, w_up, w_down, ln3, alpha, tm):
    n, d = x.shape
    tm = min(tm, n)
    n_e, _, d_ff = w_gate.shape
    return pl.pallas_call(
        functools.partial(_moe_kernel, alpha=alpha),
        grid=(n // tm, n_e),
        in_specs=[pl.BlockSpec((tm, d), lambda i, e: (i, 0)),
                  pl.BlockSpec(w_router.shape, lambda i, e: (0, 0)),
                  pl.BlockSpec(b_router.shape, lambda i, e: (0, 0)),
                  pl.BlockSpec((1, d, d_ff), lambda i, e: (e, 0, 0)),
                  pl.BlockSpec((1, d, d_ff), lambda i, e: (e, 0, 0)),
                  pl.BlockSpec((1, d_ff, d), lambda i, e: (e, 0, 0)),
                  pl.BlockSpec(ln3.shape, lambda i, e: (0, 0))],
        out_specs=pl.BlockSpec((tm, d), lambda i, e: (i, 0)),
        out_shape=jax.ShapeDtypeStruct((n, d), F32),
        scratch_shapes=[pltpu.VMEM((tm, d), BF16), pltpu.VMEM((tm, LANES), F32), pltpu.VMEM((tm, d), F32)],
        compiler_params=_cparams("parallel", "arbitrary"),
    )(x, w_router, b_router, w_gate, w_up, w_down, ln3)


def _block_diag(wh):
    h, d, _ = wh.shape
    eye = jnp.eye(h, dtype=wh.dtype)
    return (eye[:, None, :, None] * wh[:, :, None, :]).reshape(h * d, h * d)


def _layer_params(l, p):
    w = GROUP_WIDTH
    w_in = p['w_in'][l]
    cols = lambda idx: jnp.concatenate([w_in[:, i * w:(i + 1) * w] for i in idx], axis=1).astype(BF16)
    d = w_in.shape[0]
    w_router = jnp.zeros((d, LANES), F32)
    w_router = w_router.at[:, :N_EXPERTS].set(p['w_re'][l]).at[:, N_EXPERTS:N_EXPERTS + N_GROUPS].set(p['w_rg'][l])
    b_router = jnp.zeros((1, LANES), F32)
    b_router = b_router.at[0, :N_EXPERTS].set(p['b_re'][l]).at[0, N_EXPERTS:N_EXPERTS + N_GROUPS].set(p['b_rg'][l])
    w_kv = cols((3, 4, 6, 7, 9, 10))
    return dict(
        w_q=cols((0, 1, 2, 5, 8)),
        w_kv=w_kv, w_kv_t=w_kv.T,
        pvec=jnp.concatenate([p['conv_b'][l][None], p['lru_ba'][l][None], p['lru_bx'][l][None],
                              p['lru_lambda'][l][None], p['conv_w'][l]], axis=0),
        w_gates=jnp.concatenate([_block_diag(p['lru_wa'][l]), _block_diag(p['lru_wx'][l])], axis=1).astype(BF16),
        dl=p['diff_lambda'][l],
        subln=jnp.tile(p['diff_subln'][l], N_HEADS)[None],
        lam_init=0.8 - 0.6 * math.exp(-0.3 * l),
        w_out=p['w_out'][l].astype(BF16),
        ln1=jnp.stack([p['ln1_g'][l], p['ln1_b'][l]]),
        w_xq=p['w_xq'][l].astype(BF16),
        w_xo=p['w_xo'][l].astype(BF16),
        ln2=jnp.stack([p['ln2_g'][l], p['ln2_b'][l]]),
        w_router=w_router, b_router=b_router,
        w_gate=p['w_gate'][l].astype(BF16), w_up=p['w_up'][l].astype(BF16), w_down=p['w_down'][l].astype(BF16),
        ln3=jnp.stack([p['ln3_g'][l], p['ln3_b'][l]]),
        w_xkv_t=p['w_xkv'][l].astype(BF16).T,
    )


def _pack_state(conv_buf, h0):
    b, _, w = conv_buf.shape
    pad = jnp.zeros((b, SUBLANES - CONV_WIDTH, w), F32)
    return jnp.concatenate([conv_buf, h0[:, None], pad], axis=1)


def _finish_layer(x, ys, mem_kv_t, layer, lp, alpha, tm):
    b, t, d = x.shape
    x2 = _post(ys, x, lp['w_out'], lp['ln1'], lp['w_xq'], mem_kv_t, layer, lp['w_xo'], lp['ln2'], alpha, tm)
    x3 = _moe(x2.reshape(b * t, d), lp['w_router'], lp['b_router'], lp['w_gate'], lp['w_up'], lp['w_down'],
              lp['ln3'], alpha, ROW_TILE)
    return x3.reshape(b, t, d)


def _token_major(x_t, lead):
    n_slots = x_t.shape[-2] // GROUP_WIDTH
    x = x_t.reshape(x_t.shape[:lead] + (n_slots, N_HEADS, HEAD_DIM, x_t.shape[-1]))
    return jnp.moveaxis(x, -1, lead)


def kernel(x_prompt, x_sample, cache_kv, cache_mem_kv, state_rglru_h, state_conv, page_table, mem_prompt, rel_bias, w_in, conv_w, conv_b, lru_wa, lru_ba, lru_wx, lru_bx, lru_lambda, diff_lambda, diff_subln, w_out, ln1_g, ln1_b, w_xq, w_xkv, w_xo, ln2_g, ln2_b, w_rg, b_rg, w_re, b_re, w_gate, w_up, w_down, ln3_g, ln3_b):
    p = dict(w_in=w_in, conv_w=conv_w, conv_b=conv_b, lru_wa=lru_wa, lru_ba=lru_ba, lru_wx=lru_wx, lru_bx=lru_bx,
             lru_lambda=lru_lambda, diff_lambda=diff_lambda, diff_subln=diff_subln, w_out=w_out, ln1_g=ln1_g,
             ln1_b=ln1_b, w_xq=w_xq, w_xkv=w_xkv, w_xo=w_xo, ln2_g=ln2_g, ln2_b=ln2_b, w_rg=w_rg, b_rg=b_rg,
             w_re=w_re, b_re=b_re, w_gate=w_gate, w_up=w_up, w_down=w_down, ln3_g=ln3_g, ln3_b=ln3_b)
    depth = w_in.shape[0]
    alpha = (2 * depth) ** 0.25
    bp, tp, d = x_prompt.shape
    bs, ts, _ = x_sample.shape
    w = GROUP_WIDTH
    n_pool, _, page = cache_kv.shape[:3]
    assert tp % TQ == 0 and ts <= SAMPLE_ROWS and (page_table.shape[1] * page) % MOBA_BLOCK == 0
    m_len = mem_prompt.shape[1]
    cache_t = jnp.moveaxis(cache_kv, 2, -1).reshape(n_pool, depth, N_KV_SLOTS * w, page)
    mem_sample_t = jnp.moveaxis(cache_mem_kv, 2, -1).reshape(bs, depth, 2 * w, m_len)

    tab_moba, tab_diff = rel_bias[:, :N_HEADS], rel_bias[:, N_HEADS:]
    pb_moba, far_moba = _prompt_bias(tab_moba)
    pb_diff, far_diff = _prompt_bias(tab_diff)
    sb_moba = _sample_bias(tab_moba, page, ts)
    sb_diff = jnp.concatenate([_sample_bias(tab_diff, page, ts)] * 2, axis=1)

    xp = x_prompt
    xs = jnp.pad(x_sample, ((0, 0), (0, SAMPLE_ROWS - ts), (0, 0)))
    st_p = jnp.zeros((bp, SUBLANES, w), F32)
    outs = dict(kv_p=[], mem_p=[], h_p=[], conv_p=[], kv_s=[], h_s=[], conv_s=[])
    for l in range(depth):
        lp = _layer_params(l, p)
        mkv_t = _memory_kv_t(mem_prompt, lp['w_xkv_t'])

        qag, kvt, kvb = _project(xp, lp['w_q'], lp['w_kv_t'])
        y_a, st = _rglru(qag, st_p, lp['pvec'], lp['w_gates'], tc=TQ, tv=TQ)
        y_b, y_c, y_d = _prompt_mixers(qag, kvt, kvb, pb_moba, far_moba, pb_diff, far_diff,
                                       lp['dl'], lp['subln'], lp['lam_init'])
        xp = _finish_layer(xp, (y_a, y_b, y_c, y_d), mkv_t, 0, lp, alpha, ROW_TILE)
        outs['kv_p'].append(kvt)
        outs['mem_p'].append(mkv_t[:, 0])
        outs['h_p'].append(st[:, CONV_WIDTH - 1])
        outs['conv_p'].append(st[:, :CONV_WIDTH - 1])

        qag, kv = _matmul(xs.reshape(bs * SAMPLE_ROWS, d), [lp['w_q'], lp['w_kv']])
        qag, kv = qag.reshape(bs, SAMPLE_ROWS, 5 * w), kv.reshape(bs, SAMPLE_ROWS, N_KV_SLOTS * w)
        y_a, st = _rglru(qag, _pack_state(state_conv[:, l], state_rglru_h[:, l]), lp['pvec'], lp['w_gates'],
                         tc=SAMPLE_ROWS, tv=ts)
        new_page = jnp.pad(jnp.swapaxes(kv, 1, 2), ((0, 0), (0, 0), (0, page - SAMPLE_ROWS)))
        y3 = _paged_attention(qag[:, :, 2 * w:], new_page, cache_t, page_table, l, sb_moba, sb_diff,
                              lp['dl'], lp['subln'], lp['lam_init'])
        xs = _finish_layer(xs, (y_a, y3[:, :, :w], y3[:, :, w:2 * w], y3[:, :, 2 * w:]), mem_sample_t, l,
                           lp, alpha, SAMPLE_ROWS)
        outs['kv_s'].append(kv[:, :ts].reshape(bs, ts, N_KV_SLOTS, N_HEADS, HEAD_DIM))
        outs['h_s'].append(st[:, CONV_WIDTH - 1])
        outs['conv_s'].append(st[:, :CONV_WIDTH - 1])

    stack = lambda k: jnp.stack(outs[k], axis=1)
    return (xp, xs[:, :ts], _token_major(stack('kv_p'), 2), _token_major(stack('mem_p'), 2), stack('h_p'),
            stack('conv_p'), stack('kv_s'), stack('h_s'), stack('conv_s'))
```

```python
import functools
import math

import numpy as np
import jax
import jax.numpy as jnp
from jax import lax
from jax.experimental import pallas as pl
from jax.experimental.pallas import tpu as pltpu

F32 = jnp.float32
BF16 = jnp.bfloat16

HEAD_DIM = 64
N_HEADS = 4
GROUP_WIDTH = N_HEADS * HEAD_DIM
N_KV_SLOTS = 6
CONV_WIDTH = 4
LRU_C = 8.0
MOBA_BLOCK = 256
MOBA_TOPK = 3
DIFF_HALF = HEAD_DIM // 2
N_BUCKETS = 32
MAX_EXACT = N_BUCKETS // 2
MAX_DISTANCE = 128
N_GROUPS = 4
EXPERTS_PER_GROUP = 4
N_EXPERTS = N_GROUPS * EXPERTS_PER_GROUP
LN_EPS = 1e-5
NEG = -1e30
LOG2E = math.log2(math.e)
SOFTPLUS_LINEAR = 64.0
STICK_DEAD = -192.0
LANES = 128
SUBLANES = 8
VMEM_LIMIT = 56 * 1024 * 1024
TQ = MOBA_BLOCK
Q_TILES = 2
ROW_TILE = 1024
SAMPLE_ROWS = SUBLANES
PAGES_PER_STEP = 8


def _cparams(*sem):
    return pltpu.CompilerParams(dimension_semantics=sem, vmem_limit_bytes=VMEM_LIMIT)


def _softplus(x):
    return jnp.maximum(x, 0.0) + jnp.log1p(jnp.exp(-jnp.abs(x)))


def _sigmoid(x):
    return 1.0 / (1.0 + jnp.exp(-x))


def _dot(a, b, precision=None):
    return jnp.dot(a, b, preferred_element_type=F32, precision=precision)


def _dot_nt(a, b):
    return lax.dot_general(a, b, (((1,), (1,)), ((), ())), preferred_element_type=F32)


def _layer_norm(x, g, b):
    mu = jnp.mean(x, axis=-1, keepdims=True)
    xc = x - mu
    var = jnp.mean(xc * xc, axis=-1, keepdims=True)
    return xc * lax.rsqrt(var + LN_EPS) * g + b


def _bucket_table(n):
    d = np.arange(n)
    large = MAX_EXACT + (np.log(np.maximum(d, 1).astype(np.float32) / MAX_EXACT)
                         / math.log(MAX_DISTANCE / MAX_EXACT) * (N_BUCKETS - MAX_EXACT)).astype(np.int32)
    return np.where(d < MAX_EXACT, d, np.minimum(large, N_BUCKETS - 1)).astype(np.int32)


def _mm_kernel(*refs, n_out):
    x = refs[0][...].astype(BF16)
    for w_ref, o_ref in zip(refs[1:1 + n_out], refs[1 + n_out:]):
        o_ref[...] = _dot(x, w_ref[...])


def _matmul(x, ws, tm=512):
    m, k = x.shape
    tm = min(tm, m)
    return pl.pallas_call(
        functools.partial(_mm_kernel, n_out=len(ws)),
        grid=(m // tm,),
        in_specs=[pl.BlockSpec((tm, k), lambda i: (i, 0))] + [pl.BlockSpec(w.shape, lambda i: (0, 0)) for w in ws],
        out_specs=[pl.BlockSpec((tm, w.shape[1]), lambda i: (i, 0)) for w in ws],
        out_shape=[jax.ShapeDtypeStruct((m, w.shape[1]), F32) for w in ws],
        compiler_params=_cparams("parallel"),
    )(x, *ws)


def _proj_kernel(x_ref, wq_ref, wkvt_ref, q_ref, kvt_ref, kvb_ref):
    x = x_ref[0].astype(BF16)
    q_ref[0] = _dot(x, wq_ref[...])
    kvt = _dot_nt(wkvt_ref[...], x)
    kvt_ref[0] = kvt
    kvb_ref[0] = kvt.astype(BF16)


def _project(x, w_q, w_kv_t, tm=512):
    b, t, d = x.shape
    tm = min(tm, t)
    nq, nkv = w_q.shape[1], w_kv_t.shape[0]
    return pl.pallas_call(
        _proj_kernel,
        grid=(b, t // tm),
        in_specs=[pl.BlockSpec((1, tm, d), lambda i, j: (i, j, 0)),
                  pl.BlockSpec(w_q.shape, lambda i, j: (0, 0)),
                  pl.BlockSpec(w_kv_t.shape, lambda i, j: (0, 0))],
        out_specs=[pl.BlockSpec((1, tm, nq), lambda i, j: (i, j, 0)),
                   pl.BlockSpec((1, nkv, tm), lambda i, j: (i, 0, j)),
                   pl.BlockSpec((1, nkv, tm), lambda i, j: (i, 0, j))],
        out_shape=[jax.ShapeDtypeStruct((b, t, nq), F32), jax.ShapeDtypeStruct((b, nkv, t), F32),
                   jax.ShapeDtypeStruct((b, nkv, t), BF16)],
        compiler_params=_cparams("parallel", "parallel"),
    )(x, w_q, w_kv_t)


def _mem_kernel(m_ref, w_ref, o_ref):
    o_ref[0, 0] = _dot_nt(w_ref[...], m_ref[0].astype(BF16))


def _memory_kv_t(mem, w_xkv_t):
    b, m, d = mem.shape
    n = w_xkv_t.shape[0]
    return pl.pallas_call(
        _mem_kernel, grid=(b,),
        in_specs=[pl.BlockSpec((1, m, d), lambda i: (i, 0, 0)), pl.BlockSpec(w_xkv_t.shape, lambda i: (0, 0))],
        out_specs=pl.BlockSpec((1, 1, n, m), lambda i: (i, 0, 0, 0)),
        out_shape=jax.ShapeDtypeStruct((b, 1, n, m), F32),
        compiler_params=_cparams("parallel"),
    )(mem, w_xkv_t)


def _rglru_kernel(x_ref, g_ref, st_ref, pv_ref, w_ref, y_ref, so_ref, carry_ref, *, tc, tv):
    c = pl.program_id(1)

    @pl.when(c == 0)
    def _():
        carry_ref[...] = st_ref[0]

    w = GROUP_WIDTH
    x = x_ref[0]
    row = lax.broadcasted_iota(jnp.int32, (tc, w), 0)
    pv = pv_ref[...]
    conv_b, ba, bx, lam = pv[0:1], pv[1:2], pv[2:3], pv[3:4]
    prev = carry_ref[...]
    u = conv_b + pv[4 + CONV_WIDTH - 1:4 + CONV_WIDTH] * x
    for sft in range(1, CONV_WIDTH):
        xs = pltpu.roll(x, sft, 0)
        for i in range(sft):
            xs = jnp.where(row == i, prev[CONV_WIDTH - 1 + i - sft:CONV_WIDTH + i - sft], xs)
        u = u + pv[4 + CONV_WIDTH - 1 - sft:4 + CONV_WIDTH - sft] * xs
    gates = _dot(u.astype(BF16), w_ref[...])
    r = _sigmoid(gates[:, :w] + ba)
    ig = _sigmoid(gates[:, w:] + bx)
    log_a = -LRU_C * r * _softplus(-lam)
    a = jnp.exp(log_a)
    b = jnp.sqrt(-jnp.tanh(log_a) * (jnp.exp(2.0 * log_a) + 1.0)) * ig * u
    s = 1
    while s < tc:
        a_s = pltpu.roll(a, s, 0)
        b_s = pltpu.roll(b, s, 0)
        m = row >= s
        b = jnp.where(m, a * b_s + b, b)
        a = jnp.where(m, a * a_s, a)
        s *= 2
    h = b + a * prev[CONV_WIDTH - 1:CONV_WIDTH]
    gt = g_ref[0]
    gelu = 0.5 * gt * (1.0 + jnp.tanh(math.sqrt(2.0 / math.pi) * (gt + 0.044715 * (gt * gt * gt))))
    y_ref[0] = h * gelu
    carry_ref[0:CONV_WIDTH - 1] = x[tv - (CONV_WIDTH - 1):tv]
    carry_ref[CONV_WIDTH - 1:CONV_WIDTH] = h[tv - 1:tv]

    @pl.when(c == pl.num_programs(1) - 1)
    def _():
        so_ref[0] = carry_ref[...]


def _rglru(qag, state, pvec, w_gates, tc, tv):
    b, t, _ = qag.shape
    w = GROUP_WIDTH
    assert t % tc == 0 and tv >= CONV_WIDTH - 1
    return pl.pallas_call(
        functools.partial(_rglru_kernel, tc=tc, tv=tv),
        grid=(b, t // tc),
        in_specs=[pl.BlockSpec((1, tc, w), lambda i, c: (i, c, 0)),
                  pl.BlockSpec((1, tc, w), lambda i, c: (i, c, 1)),
                  pl.BlockSpec((1, SUBLANES, w), lambda i, c: (i, 0, 0)),
                  pl.BlockSpec((SUBLANES, w), lambda i, c: (0, 0)),
                  pl.BlockSpec((w, 2 * w), lambda i, c: (0, 0))],
        out_specs=[pl.BlockSpec((1, tc, w), lambda i, c: (i, c, 0)),
                   pl.BlockSpec((1, SUBLANES, w), lambda i, c: (i, 0, 0))],
        out_shape=[jax.ShapeDtypeStruct((b, t, w), F32), jax.ShapeDtypeStruct((b, SUBLANES, w), F32)],
        scratch_shapes=[pltpu.VMEM((SUBLANES, w), F32)],
        compiler_params=_cparams("parallel", "arbitrary"),
    )(qag, qag, state, pvec, w_gates)


def _tri_tables(n_super):
    qs, ks = [], []
    for qs_i in range(n_super):
        for ki in range(Q_TILES * qs_i + Q_TILES - 1, -1, -1):
            qs.append(qs_i)
            ks.append(ki)
    return jnp.asarray(np.array(qs, np.int32)), jnp.asarray(np.array(ks, np.int32))


def _wide(x):
    return jnp.concatenate([x, x], axis=1)


def _values_with(vt, h, fill):
    rowh = lax.broadcasted_iota(jnp.int32, vt.shape, 0) // HEAD_DIM
    return jnp.where(rowh == h, vt, jnp.full((), fill, vt.dtype))


def _softmax_update(j, rows, t2, m_cand, vt_aug, m_ref, acc_ref, shift=None):
    m_old = m_ref[j, rows, :]
    m_new = jnp.maximum(m_old, m_cand)
    off = m_new if shift is None else m_new - shift
    p = jnp.exp2(t2 - _wide(off))
    acc_ref[j, rows, :] = jnp.exp2(m_old - m_new) * acc_ref[j, rows, :] + _dot_nt(p.astype(BF16), vt_aug)
    m_ref[j, rows, :] = m_new


def _normalised(acc):
    return acc / pltpu.roll(acc, HEAD_DIM, 1)


def _moba_select(q, km, q_blk):
    blk = lax.broadcasted_iota(jnp.int32, q_blk.shape, 1)
    blkf = blk.astype(F32)
    gate = _dot(q, km, precision=lax.Precision.HIGHEST)
    g = jnp.where(blk < q_blk, gate, -jnp.inf)
    sel = jnp.zeros(q_blk.shape, F32)
    for _ in range(MOBA_TOPK):
        mx = jnp.max(g, axis=1, keepdims=True)
        idx = jnp.min(jnp.where(g == mx, blkf, float(LANES)), axis=1, keepdims=True)
        hit = blkf == idx
        sel = jnp.where(hit, 1.0, sel)
        g = jnp.where(hit, -jnp.inf, g)
    return jnp.where(blk < q_blk, sel, 0.0)


def _diff_lambda(dl, lam_init):
    return (jnp.exp(jnp.sum(dl[0:1] * dl[1:2], axis=1, keepdims=True))
            - jnp.exp(jnp.sum(dl[2:3] * dl[3:4], axis=1, keepdims=True)) + lam_init)


def _mix_kernel(qt_ref, kt_ref, qb_ref, qc_ref, qd_ref, kv_ref,
                ntri_ref, km_ref, bias_b_ref, far_b_ref, bias_d_ref, far_d_ref, dl_ref, sub_ref,
                ob_ref, oc_ref, od_ref,
                qmb_ref, sel_ref, mb_ref, accb_ref, qmc_ref, run_ref, accc_ref, qmd_ref, md_ref, accd_ref,
                live_ref, *, lam_init):
    t = pl.program_id(2)
    qs, ki = qt_ref[t], kt_ref[t]
    off = ki - Q_TILES * qs

    @pl.when(off == Q_TILES - 1)
    def _():
        qb = qb_ref[0]
        qb2 = qb * (HEAD_DIM ** -0.5 * LOG2E)
        qc2 = qc_ref[0] * (HEAD_DIM ** -0.5 * LOG2E)
        qd2 = qd_ref[0] * (DIFF_HALF ** -0.5 * LOG2E)
        q_blk = Q_TILES * qs + lax.broadcasted_iota(jnp.int32, (Q_TILES * TQ, LANES), 0) // TQ
        for h in range(2):
            hs = slice(h * HEAD_DIM, (h + 1) * HEAD_DIM)
            sel_ref[:, h * LANES:(h + 1) * LANES] = _moba_select(qb[:, hs], km_ref[0, hs, :], q_blk).astype(BF16)
            qmb_ref[h] = qb2[:, hs].astype(BF16)
            qmc_ref[h] = qc2[:, hs].astype(BF16)
        for j in range(4):
            qmd_ref[j] = qd2[:, j * DIFF_HALF:(j + 1) * DIFF_HALF].astype(BF16)
        mb_ref[...] = jnp.full_like(mb_ref, NEG)
        md_ref[...] = jnp.full_like(md_ref, NEG)
        accb_ref[...] = jnp.zeros_like(accb_ref)
        accc_ref[...] = jnp.zeros_like(accc_ref)
        accd_ref[...] = jnp.zeros_like(accd_ref)
        run_ref[...] = jnp.zeros_like(run_ref)
        live_ref[0] = 1


    def moba_step(rows, dist):
        kt = kv_ref[0, 0, 0]
        vt = kv_ref[0, 1, 0]
        if dist != 0:
            hit = lax.broadcasted_iota(jnp.int32, (LANES, LANES), 0) == ki
            e = jnp.where(hit, 1.0, 0.0).astype(BF16)
            z = jnp.zeros_like(e)
            onehot = jnp.concatenate([jnp.concatenate([e, z], axis=1), jnp.concatenate([z, e], axis=1)], axis=0)
            picked = _dot(sel_ref[rows, :], onehot)
        for h in range(2):
            raw = _dot(qmb_ref[h, rows, :], kt[h * HEAD_DIM:(h + 1) * HEAD_DIM])
            vt_aug = _values_with(vt, h, 1.0)
            row_pen = None if dist == 0 else jnp.where(picked[:, h * LANES:(h + 1) * LANES] > 0.5, 0.0, NEG)
            if dist is not None:
                t2 = raw + bias_b_ref[h, dist]
                m_cand = jnp.max(t2, axis=1, keepdims=True)
                m_cand = m_cand if dist == 0 else m_cand + row_pen
                _softmax_update(h, rows, t2, m_cand, vt_aug, mb_ref, accb_ref, shift=row_pen)
            else:
                shift = far_b_ref[h, 0:1] + row_pen
                m_cand = jnp.max(raw, axis=1, keepdims=True) + shift
                _softmax_update(h, rows, raw, m_cand, vt_aug, mb_ref, accb_ref, shift=shift)

    def stick_step(rows, dist):
        kt = kv_ref[0, 2, 0]
        vt = kv_ref[0, 3, 0]
        n = rows.stop - rows.start
        if dist == 0:
            r = lax.broadcasted_iota(jnp.int32, (TQ, TQ), 0)
            c = lax.broadcasted_iota(jnp.int32, (TQ, TQ), 1)
            valid = c < r
        z2s, runs, parts = [], [], []
        for h in range(2):
            z2 = _dot(qmc_ref[h, rows, :], kt[h * HEAD_DIM:(h + 1) * HEAD_DIM])
            sp2 = jnp.where(z2 > SOFTPLUS_LINEAR, z2, jnp.log2(1.0 + jnp.exp2(z2)))
            if dist == 0:
                sp2 = jnp.where(valid, sp2, 0.0)
            z2s.append(z2)
            parts.append(sp2.astype(BF16))
            runs.append(run_ref[h, rows, :] - jnp.broadcast_to(jnp.sum(sp2, axis=1, keepdims=True), (n, LANES)))
        cums = _dot(jnp.concatenate(parts, axis=0), ntri_ref[...])
        acc = accc_ref[rows, :]
        for h in range(2):
            e = jnp.exp2(z2s[h] + cums[h * n:(h + 1) * n] + _wide(run_ref[h, rows, :]))
            if dist == 0:
                e = jnp.where(valid, e, 0.0)
            acc = acc + _dot_nt(e.astype(BF16), _values_with(vt, h, 0.0))
            run_ref[h, rows, :] = runs[h]
        accc_ref[rows, :] = acc

    def diff_step(rows, dist):
        kt = kv_ref[0, 4, 0]
        vt = kv_ref[0, 5, 0]
        vt_aug = [_values_with(vt, h, 1.0) for h in range(2)]
        for j in range(4):
            h = j // 2
            raw = _dot(qmd_ref[j, rows, :], kt[j * DIFF_HALF:(j + 1) * DIFF_HALF])
            if dist is not None:
                t2 = raw + bias_d_ref[h, dist]
                _softmax_update(j, rows, t2, jnp.max(t2, axis=1, keepdims=True), vt_aug[h], md_ref, accd_ref)
            else:
                shift = far_d_ref[h, 0:1]
                m_cand = jnp.max(raw, axis=1, keepdims=True) + shift
                _softmax_update(j, rows, raw, m_cand, vt_aug[h], md_ref, accd_ref, shift=shift)

    def all_mixers(rows, dist):
        moba_step(rows, dist)
        stick_step(rows, dist)
        diff_step(rows, dist)

    def near_diagonal_step(o):
        for r in range(max(o, 0), min(o + 2, Q_TILES)):
            all_mixers(slice(r * TQ, (r + 1) * TQ), r - o)
        if o + 2 < Q_TILES:
            all_mixers(slice((o + 2) * TQ, Q_TILES * TQ), None)

    for o in range(Q_TILES - 1, -2, -1):
        pl.when(off == o)(functools.partial(near_diagonal_step, o))

    @pl.when(off <= -2)
    def _():
        rows = slice(0, Q_TILES * TQ)
        moba_step(rows, None)
        diff_step(rows, None)

        @pl.when(live_ref[0] == 1)
        def _():
            stick_step(rows, None)
            live_ref[0] = (jnp.max(run_ref[...]) >= STICK_DEAD).astype(jnp.int32)

    @pl.when(ki == 0)
    def _():
        head = lax.broadcasted_iota(jnp.int32, (Q_TILES * TQ, LANES), 1) // HEAD_DIM
        ob_ref[0] = jnp.where(head == 0, _normalised(accb_ref[0]), _normalised(accb_ref[1]))
        oc_ref[0] = accc_ref[...]
        lam = _diff_lambda(dl_ref[...], lam_init)
        o = jnp.where(head == 0,
                      _normalised(accd_ref[0]) - lam * _normalised(accd_ref[1]),
                      _normalised(accd_ref[2]) - lam * _normalised(accd_ref[3]))
        o2 = o * o
        ms = jnp.where(head == 0,
                       jnp.sum(jnp.where(head == 0, o2, 0.0), axis=1, keepdims=True),
                       jnp.sum(jnp.where(head == 1, o2, 0.0), axis=1, keepdims=True)) * (1.0 / HEAD_DIM)
        od_ref[0] = o * lax.rsqrt(ms + LN_EPS) * sub_ref[...] * (1.0 - lam_init)


def _kmean_kernel(k_ref, ones_ref, o_ref):
    o_ref[0] = _dot(k_ref[0], ones_ref[...], precision=lax.Precision.HIGHEST)


def _suffix_tri(n, sign=1.0):
    i = np.arange(n)
    return jnp.asarray(sign * (i[:, None] >= i[None, :]).astype(np.float32), dtype=BF16)


def _prompt_mixers(qag, kvt, kvb, bias_b, far_b, bias_d, far_d, dl, subln, lam_init):
    b, t, _ = qag.shape
    kv5 = kvb.reshape(b, N_KV_SLOTS, 2, LANES, t)
    nb = t // MOBA_BLOCK
    assert nb <= LANES
    blk_mean = np.zeros((t, LANES), np.float32)
    blk_mean[np.arange(t), np.arange(t) // MOBA_BLOCK] = 1.0 / MOBA_BLOCK
    kmean_t = pl.pallas_call(
        _kmean_kernel, grid=(b,),
        in_specs=[pl.BlockSpec((1, GROUP_WIDTH, t), lambda i: (i, 0, 0)),
                  pl.BlockSpec((t, LANES), lambda i: (0, 0))],
        out_specs=pl.BlockSpec((1, GROUP_WIDTH, LANES), lambda i: (i, 0, 0)),
        out_shape=jax.ShapeDtypeStruct((b, GROUP_WIDTH, LANES), F32),
        compiler_params=_cparams("parallel"),
    )(kvt, jnp.asarray(blk_mean))
    rows = Q_TILES * TQ
    assert t % rows == 0
    qt, kt = _tri_tables(t // rows)
    q_spec = lambda col: pl.BlockSpec((1, rows, LANES), lambda b, p, t, qt, kt: (b, qt[t], col + p))
    kv_spec = pl.BlockSpec((1, N_KV_SLOTS, 1, LANES, TQ), lambda b, p, t, qt, kt: (b, 0, p, 0, kt[t]))
    bias_spec = pl.BlockSpec((2, 2, TQ, TQ), lambda b, p, t, qt, kt: (p, 0, 0, 0))
    far_spec = pl.BlockSpec((2, SUBLANES, LANES), lambda b, p, t, qt, kt: (p, 0, 0))
    const2 = lambda shape: pl.BlockSpec(shape, lambda b, p, t, qt, kt: (0, 0))
    out_spec = pl.BlockSpec((1, rows, LANES), lambda b, p, t, qt, kt: (b, qt[t], p))
    out = jax.ShapeDtypeStruct((b, t, GROUP_WIDTH), F32)
    stats = lambda n: pltpu.VMEM((n, rows, LANES), F32)
    return pl.pallas_call(
        functools.partial(_mix_kernel, lam_init=lam_init),
        grid_spec=pltpu.PrefetchScalarGridSpec(
            num_scalar_prefetch=2, grid=(b, 2, qt.shape[0]),
            in_specs=[q_spec(4), q_spec(6), q_spec(8), kv_spec]
                     + [const2((TQ, TQ)), pl.BlockSpec((1, LANES, LANES), lambda b, p, t, qt, kt: (b, p, 0)),
                        bias_spec, far_spec, bias_spec, far_spec, const2(dl.shape), const2((1, LANES))],
            out_specs=[out_spec] * 3,
            scratch_shapes=[pltpu.VMEM((2, rows, HEAD_DIM), BF16), pltpu.VMEM((rows, 2 * LANES), BF16),
                            stats(2), stats(2),
                            pltpu.VMEM((2, rows, HEAD_DIM), BF16), stats(2), pltpu.VMEM((rows, LANES), F32),
                            pltpu.VMEM((4, rows, DIFF_HALF), BF16), stats(4), stats(4),
                            pltpu.SMEM((1,), jnp.int32)]),
        out_shape=[out] * 3,
        compiler_params=_cparams("parallel", "parallel", "arbitrary"),
    )(qt, kt, qag, qag, qag, kv5, _suffix_tri(TQ, -1.0), kmean_t,
      bias_b, far_b, bias_d, far_d, dl, subln[:, :LANES])


def _toeplitz(g):
    h, l = g.shape
    n = l // 2
    w = jnp.concatenate([g[:, :1], jnp.flip(g[:, 1:], axis=1)], axis=1)
    rep = jnp.tile(w, (1, n))[:, :n * (l - 1)].reshape(h, n, l - 1)
    return rep[:, :, :n]


def _prompt_bias(tab):
    n = TQ
    bucket = _bucket_table(4 * n)
    assert (bucket[n + 1:] == bucket[-1]).all()
    x = np.arange(2 * n)
    d = np.where(x < n, x, x - 2 * n)
    fvec = (tab * LOG2E).T
    g0 = jnp.where(jnp.asarray(d >= 0), fvec[:, bucket[np.maximum(d, 0)]], NEG)
    g1 = fvec[:, bucket[n + d]]
    tiles = jnp.stack([_toeplitz(g0), _toeplitz(g1)], axis=1)
    far = jnp.broadcast_to(fvec[:, bucket[-1]][:, None, None], (tab.shape[1], SUBLANES, LANES))
    return tiles, far


def _expand_rows(x4):
    x = jnp.concatenate([x4] * N_HEADS, axis=0)
    rh = lax.broadcasted_iota(jnp.int32, x.shape, 0) // SAMPLE_ROWS
    lh = lax.broadcasted_iota(jnp.int32, x.shape, 1) // HEAD_DIM
    return jnp.where(rh == lh, x, 0.0)


def _collapse_rows(x):
    rh = lax.broadcasted_iota(jnp.int32, x.shape, 0) // SAMPLE_ROWS
    lh = lax.broadcasted_iota(jnp.int32, x.shape, 1) // HEAD_DIM
    x = jnp.where(rh == lh, x, 0.0)
    out = x[0:SAMPLE_ROWS]
    for h in range(1, N_HEADS):
        out = out + x[h * SAMPLE_ROWS:(h + 1) * SAMPLE_ROWS]
    return out


def _paged_kernel(pt_ref, q_ref, new_ref, *rest, g_pages, lam_init):
    page_refs = rest[:g_pages]
    bm_ref, bd_ref, tri_ref, dl_ref, sub_ref, o_ref = rest[g_pages:g_pages + 6]
    (qb_ref, qc_ref, qd_ref, mb_m, mb_l, mb_g, mb_acc,
     run_ref, accc_ref, md_ref, ld_ref, accd_ref, live_ref) = rest[g_pages + 6:]
    s_idx = pl.program_id(1)
    n_steps = pl.num_programs(1)
    nblk = mb_m.shape[0] - 1
    w = GROUP_WIDTH
    rows = N_HEADS * SAMPLE_ROWS
    page = new_ref.shape[2]
    qq = lax.broadcasted_iota(jnp.int32, (rows, page), 0) % SAMPLE_ROWS
    kk = lax.broadcasted_iota(jnp.int32, (rows, page), 1)
    tri = tri_ref[...]

    def slot(pg, i):
        return pg[i * w:(i + 1) * w, :].astype(BF16)

    def moba_block(pages, biases, pens, n):
        ss, gsum = [], None
        for pg, bias, pen in zip(pages, biases, pens):
            raw = _dot(qb_ref[...], slot(pg, 0))
            sc = raw + bias
            ss.append(sc if pen is None else sc + pen)
            gsum = raw if gsum is None else gsum + raw
        m = ss[0].max(axis=1, keepdims=True)
        for sc in ss[1:]:
            m = jnp.maximum(m, sc.max(axis=1, keepdims=True))
        psum, acc = None, None
        for pg, sc in zip(pages, ss):
            p = jnp.exp(sc - m)
            pa = _dot_nt(p.astype(BF16), slot(pg, 1))
            psum = p if psum is None else psum + p
            acc = pa if acc is None else acc + pa
        mb_m[n] = m
        mb_l[n] = jnp.sum(psum, axis=1, keepdims=True)
        mb_acc[n] = acc
        mb_g[n] = jnp.sum(gsum, axis=1, keepdims=True)

    def stick_page(pg, valid):
        z = _dot(qc_ref[...], slot(pg, 2))
        lk = -_softplus(z)
        if valid is not None:
            lk = jnp.where(valid, lk, 0.0)
        hi = lk.astype(BF16)
        lo = (lk - hi.astype(F32)).astype(BF16)
        cum = _dot(hi, tri) + _dot(lo, tri)
        e = jnp.exp(z + cum + run_ref[...])
        if valid is not None:
            e = jnp.where(valid, e, 0.0)
        accc_ref[...] += _dot_nt(e.astype(BF16), slot(pg, 3))
        run_ref[...] += jnp.sum(lk, axis=1, keepdims=True)

    def diff_pages(pages, biases, pens):
        ss = []
        for pg, bias, pen in zip(pages, biases, pens):
            sc = _dot(qd_ref[...], slot(pg, 4)) * DIFF_HALF ** -0.5 + bias
            ss.append(sc if pen is None else sc + pen)
        mx = ss[0]
        for sc in ss[1:]:
            mx = jnp.maximum(mx, sc)
        m_old = md_ref[...]
        m_new = jnp.maximum(m_old, mx.max(axis=1, keepdims=True))
        alpha = jnp.exp(m_old - m_new)
        psum, acc = None, None
        for pg, sc in zip(pages, ss):
            p = jnp.exp(sc - m_new)
            pa = _dot_nt(p.astype(BF16), slot(pg, 5))
            psum = p if psum is None else psum + p
            acc = pa if acc is None else acc + pa
        ld_ref[...] = alpha * ld_ref[...] + jnp.sum(psum, axis=1, keepdims=True)
        accd_ref[...] = alpha * accd_ref[...] + acc
        md_ref[...] = m_new

    @pl.when(s_idx == 0)
    def _():
        q = q_ref[0]
        qb_ref[...] = (_expand_rows(q[:, 0:w]) * HEAD_DIM ** -0.5).astype(BF16)
        qc_ref[...] = (_expand_rows(q[:, w:2 * w]) * HEAD_DIM ** -0.5).astype(BF16)
        qdx = _expand_rows(q[:, 2 * w:3 * w])
        half = lax.broadcasted_iota(jnp.int32, qdx.shape, 1) // DIFF_HALF % 2
        qd_ref[0:rows] = jnp.where(half == 0, qdx, 0.0).astype(BF16)
        qd_ref[rows:2 * rows] = jnp.where(half == 1, qdx, 0.0).astype(BF16)
        run_ref[...] = jnp.zeros_like(run_ref)
        live_ref[0] = 1
        accc_ref[...] = jnp.zeros_like(accc_ref)
        md_ref[...] = jnp.full_like(md_ref, NEG)
        ld_ref[...] = jnp.zeros_like(ld_ref)
        accd_ref[...] = jnp.zeros_like(accd_ref)
        new = new_ref[0]
        causal_pen = jnp.where(kk <= qq, 0.0, NEG)
        moba_block([new], [bm_ref[0]], [causal_pen], nblk)
        stick_page(new, kk < qq)
        diff_pages([new], [bd_ref[0]], [jnp.concatenate([causal_pen, causal_pen], axis=0)])

    pages = [r[0, 0] for r in page_refs]
    bm0 = jnp.where(s_idx == 0, bm_ref[1], bm_ref[2])
    bd0 = jnp.where(s_idx == 0, bd_ref[1], bd_ref[2])
    bms = [bm0] + [bm_ref[2]] * (g_pages - 1)
    bds = [bd0] + [bd_ref[2]] * (g_pages - 1)
    per_block = MOBA_BLOCK // page
    for i in range(0, g_pages, per_block):
        n = nblk - 1 - (s_idx * g_pages + i) // per_block
        moba_block(pages[i:i + per_block], bms[i:i + per_block], [None] * per_block, n)
    diff_pages(pages, bds, [None] * g_pages)

    @pl.when(live_ref[0] == 1)
    def _():
        for pg in pages:
            stick_page(pg, None)
        live_ref[0] = (jnp.max(run_ref[...]) >= STICK_DEAD / LOG2E).astype(jnp.int32)

    @pl.when(s_idx == n_steps - 1)
    def _():
        blkf = lax.broadcasted_iota(jnp.int32, (nblk, rows, 1), 0).astype(F32)
        g = mb_g[0:nblk]
        sel = jnp.zeros((nblk, rows, 1), F32)
        for _ in range(min(MOBA_TOPK, nblk)):
            mx = jnp.max(g, axis=0, keepdims=True)
            idx = jnp.min(jnp.where(g == mx, blkf, float(nblk)), axis=0, keepdims=True)
            hit = blkf == idx
            sel = jnp.where(hit, 1.0, sel)
            g = jnp.where(hit, -jnp.inf, g)
        m_own = mb_m[nblk]
        m_tot = jnp.maximum(m_own, jnp.max(jnp.where(sel > 0.5, mb_m[0:nblk], NEG), axis=0))
        wgt = jnp.where(sel > 0.5, jnp.exp(mb_m[0:nblk] - m_tot), 0.0)
        w_own = jnp.exp(m_own - m_tot)
        l_tot = w_own * mb_l[nblk] + jnp.sum(wgt * mb_l[0:nblk], axis=0)
        a_tot = w_own * mb_acc[nblk] + jnp.sum(wgt * mb_acc[0:nblk], axis=0)
        o_ref[0, :, 0:w] = _collapse_rows(a_tot / l_tot)
        o_ref[0, :, w:2 * w] = _collapse_rows(accc_ref[...])
        lam = _diff_lambda(dl_ref[...], lam_init)
        od = (accd_ref[0:rows] / ld_ref[0:rows] - lam * (accd_ref[rows:2 * rows] / ld_ref[rows:2 * rows]))
        od = _collapse_rows(od)
        od2 = od * od
        lh = lax.broadcasted_iota(jnp.int32, od.shape, 1) // HEAD_DIM
        ms = jnp.zeros_like(od)
        for h in range(N_HEADS):
            ms = jnp.where(lh == h, jnp.sum(jnp.where(lh == h, od2, 0.0), axis=1, keepdims=True), ms)
        o_ref[0, :, 2 * w:3 * w] = od * lax.rsqrt(ms * (1.0 / HEAD_DIM) + LN_EPS) * sub_ref[...] * (1.0 - lam_init)


def _paged_attention(q3, new_page, cache_t, page_table, layer, bm, bd, dl, subln, lam_init):
    b = q3.shape[0]
    n_pages = page_table.shape[1]
    page = cache_t.shape[3]
    g_pages = min(PAGES_PER_STEP, n_pages)
    assert n_pages % g_pages == 0 and MOBA_BLOCK % page == 0 and g_pages % (MOBA_BLOCK // page) == 0
    nblk = n_pages * page // MOBA_BLOCK
    rows = N_HEADS * SAMPLE_ROWS
    w = GROUP_WIDTH

    def page_spec(i):
        return pl.BlockSpec((1, 1, N_KV_SLOTS * w, page),
                            lambda bi, s, pt: (pt[bi, n_pages - 1 - (s * g_pages + i)], layer, 0, 0))

    const2 = lambda bi, s, pt: (0, 0)
    const3 = lambda bi, s, pt: (0, 0, 0)
    return pl.pallas_call(
        functools.partial(_paged_kernel, g_pages=g_pages, lam_init=lam_init),
        grid_spec=pltpu.PrefetchScalarGridSpec(
            num_scalar_prefetch=1, grid=(b, n_pages // g_pages),
            in_specs=[pl.BlockSpec((1, SAMPLE_ROWS, 3 * w), lambda bi, s, pt: (bi, 0, 0)),
                      pl.BlockSpec((1, N_KV_SLOTS * w, page), lambda bi, s, pt: (bi, 0, 0))]
                     + [page_spec(i) for i in range(g_pages)]
                     + [pl.BlockSpec(bm.shape, const3), pl.BlockSpec(bd.shape, const3),
                        pl.BlockSpec((page, page), const2), pl.BlockSpec(dl.shape, const2),
                        pl.BlockSpec((1, w), const2)],
            out_specs=pl.BlockSpec((1, SAMPLE_ROWS, 3 * w), lambda bi, s, pt: (bi, 0, 0)),
            scratch_shapes=[pltpu.VMEM((rows, w), BF16), pltpu.VMEM((rows, w), BF16),
                            pltpu.VMEM((2 * rows, w), BF16),
                            pltpu.VMEM((nblk + 1, rows, 1), F32), pltpu.VMEM((nblk + 1, rows, 1), F32),
                            pltpu.VMEM((nblk + 1, rows, 1), F32), pltpu.VMEM((nblk + 1, rows, w), F32),
                            pltpu.VMEM((rows, 1), F32), pltpu.VMEM((rows, w), F32),
                            pltpu.VMEM((2 * rows, 1), F32), pltpu.VMEM((2 * rows, 1), F32),
                            pltpu.VMEM((2 * rows, w), F32), pltpu.SMEM((1,), jnp.int32)]),
        out_shape=jax.ShapeDtypeStruct((b, SAMPLE_ROWS, 3 * w), F32),
        compiler_params=_cparams("parallel", "arbitrary"),
    )(page_table, q3, new_page, *([cache_t] * g_pages), bm, bd, _suffix_tri(page), dl, subln)


def _sample_bias(tab, page, ts):
    bucket = _bucket_table(2 * page + SAMPLE_ROWS)
    assert (bucket[page + 1:] == bucket[-1]).all()
    q = np.minimum(np.arange(SAMPLE_ROWS), ts - 1)
    k = np.arange(page)
    idx = np.stack([bucket[np.maximum(q[:, None] - k[None, :], 0)],
                    bucket[page + q[:, None] - k[None, :]],
                    np.full((SAMPLE_ROWS, page), bucket[-1])])
    b = jnp.transpose(tab[jnp.asarray(idx)], (0, 3, 1, 2))
    return b.reshape(3, N_HEADS * SAMPLE_ROWS, page)


def _post_kernel(ya_ref, yb_ref, yc_ref, yd_ref, x_ref, wo_ref, ln1_ref, wq_ref, mem_ref, wxo_ref, ln2_ref,
                 o_ref, *, alpha):
    w = GROUP_WIDTH
    mixed = None
    for i, y_ref in enumerate((ya_ref, yb_ref, yc_ref, yd_ref)):
        part = _dot(y_ref[0].astype(BF16), wo_ref[i * w:(i + 1) * w, :])
        mixed = part if mixed is None else mixed + part
    x1 = _layer_norm(alpha * x_ref[0] + mixed, ln1_ref[0:1], ln1_ref[1:2])
    qx = _dot(x1.astype(BF16), wq_ref[...])
    kt = mem_ref[0, 0, :w, :].astype(BF16)
    vt = mem_ref[0, 0, w:, :].astype(BF16)
    head = lax.broadcasted_iota(jnp.int32, qx.shape, 1) // HEAD_DIM
    o = jnp.zeros_like(qx)
    for h in range(N_HEADS):
        s = _dot(jnp.where(head == h, qx, 0.0).astype(BF16), kt) * HEAD_DIM ** -0.5
        p = jnp.exp(s - jnp.max(s, axis=1, keepdims=True))
        p = p / jnp.sum(p, axis=1, keepdims=True)
        o = jnp.where(head == h, _dot_nt(p.astype(BF16), vt), o)
    xo = _dot(o.astype(BF16), wxo_ref[...])
    o_ref[0] = _layer_norm(alpha * x1 + xo, ln2_ref[0:1], ln2_ref[1:2])


def _post(ys, x, w_out, ln1, w_xq, mem_kv_t, layer, w_xo, ln2, alpha, tm):
    b, t, d = x.shape
    w = GROUP_WIDTH
    tm = min(tm, t)
    m_len = mem_kv_t.shape[3]
    row = lambda width: pl.BlockSpec((1, tm, width), lambda i, j: (i, j, 0))
    const = lambda shape: pl.BlockSpec(shape, lambda i, j: (0,) * len(shape))
    return pl.pallas_call(
        functools.partial(_post_kernel, alpha=alpha),
        grid=(b, t // tm),
        in_specs=[row(w)] * 4 + [row(d), const(w_out.shape), const(ln1.shape), const(w_xq.shape),
                                 pl.BlockSpec((1, 1, 2 * w, m_len), lambda i, j: (i, layer, 0, 0)),
                                 const(w_xo.shape), const(ln2.shape)],
        out_specs=row(d),
        out_shape=jax.ShapeDtypeStruct((b, t, d), F32),
        compiler_params=_cparams("parallel", "parallel"),
    )(*ys, x, w_out, ln1, w_xq, mem_kv_t, w_xo, ln2)


def _route(x, wr_ref, br_ref):
    logits = _dot(x, wr_ref[...], precision=lax.Precision.HIGHEST) + br_ref[...]
    lane = lax.broadcasted_iota(jnp.int32, logits.shape, 1)
    lanef = lane.astype(F32)
    is_grp = jnp.logical_and(lane >= N_EXPERTS, lane < N_EXPERTS + N_GROUPS)
    lg = jnp.where(is_grp, logits, -jnp.inf)
    gmax = jnp.max(lg, axis=1, keepdims=True)
    pg_sel = 1.0 / jnp.sum(jnp.exp(lg - gmax), axis=1, keepdims=True)
    gsel = jnp.min(jnp.where(lg == gmax, lanef, float(LANES)), axis=1, keepdims=True) - N_EXPERTS
    in_grp = jnp.logical_and(lane < N_EXPERTS, (lane // EXPERTS_PER_GROUP).astype(F32) == gsel)
    le = jnp.where(in_grp, logits, -jnp.inf)
    pe = jnp.exp(le - jnp.max(le, axis=1, keepdims=True))
    pe = pe / jnp.sum(pe, axis=1, keepdims=True)
    cand = jnp.where(in_grp, pe, -1.0)
    p1 = jnp.max(cand, axis=1, keepdims=True)
    i1 = jnp.min(jnp.where(cand == p1, lanef, float(LANES)), axis=1, keepdims=True)
    cand = jnp.where(lanef == i1, -1.0, cand)
    p2 = jnp.max(cand, axis=1, keepdims=True)
    i2 = jnp.min(jnp.where(cand == p2, lanef, float(LANES)), axis=1, keepdims=True)
    top = jnp.where(lanef == i1, p1, 0.0) + jnp.where(lanef == i2, p2, 0.0)
    return pg_sel * top / (p1 + p2)


def _moe_kernel(x_ref, wr_ref, br_ref, wg_ref, wu_ref, wd_ref, ln_ref, o_ref, xb_ref, cw_ref, acc_ref, *, alpha):
    e = pl.program_id(1)

    @pl.when(e == 0)
    def _():
        x = x_ref[...]
        xb_ref[...] = x.astype(BF16)
        cw_ref[...] = _route(x, wr_ref, br_ref)
        acc_ref[...] = jnp.zeros_like(acc_ref)

    xb = xb_ref[...]
    gate = _dot(xb, wg_ref[0])
    hid = gate * _sigmoid(gate) * _dot(xb, wu_ref[0])
    y = _dot(hid.astype(BF16), wd_ref[0])
    lane = lax.broadcasted_iota(jnp.int32, cw_ref.shape, 1)
    cw = jnp.sum(jnp.where(lane == e, cw_ref[...], 0.0), axis=1, keepdims=True)
    acc_ref[...] += cw * y

    @pl.when(e == pl.num_programs(1) - 1)
    def _():
        o_ref[...] = _layer_norm(alpha * x_ref[...] + acc_ref[...], ln_ref[0:1], ln_ref[1:2])


def _moe(x, w_router, b_router, w_gate, w_up, w_down, ln3, alpha, tm):
    n, d = x.shape
    tm = min(tm, n)
    n_e, _, d_ff = w_gate.shape
    return pl.pallas_call(
        functools.partial(_moe_kernel, alpha=alpha),
        grid=(n // tm, n_e),
        in_specs=[pl.BlockSpec((tm, d), lambda i, e: (i, 0)),
                  pl.BlockSpec(w_router.shape, lambda i, e: (0, 0)),
                  pl.BlockSpec(b_router.shape, lambda i, e: (0, 0)),
                  pl.BlockSpec((1, d, d_ff), lambda i, e: (e, 0, 0)),
                  pl.BlockSpec((1, d, d_ff), lambda i, e: (e, 0, 0)),
                  pl.BlockSpec((1, d_ff, d), lambda i, e: (e, 0, 0)),
                  pl.BlockSpec(ln3.shape, lambda i, e: (0, 0))],
        out_specs=pl.BlockSpec((tm, d), lambda i, e: (i, 0)),
        out_shape=jax.ShapeDtypeStruct((n, d), F32),
        scratch_shapes=[pltpu.VMEM((tm, d), BF16), pltpu.VMEM((tm, LANES), F32), pltpu.VMEM((tm, d), F32)],
        compiler_params=_cparams("parallel", "arbitrary"),
    )(x, w_router, b_router, w_gate, w_up, w_down, ln3)


def _block_diag(wh):
    h, d, _ = wh.shape
    eye = jnp.eye(h, dtype=wh.dtype)
    return (eye[:, None, :, None] * wh[:, :, None, :]).reshape(h * d, h * d)


def _layer_params(l, p):
    w = GROUP_WIDTH
    w_in = p['w_in'][l]
    cols = lambda idx: jnp.concatenate([w_in[:, i * w:(i + 1) * w] for i in idx], axis=1).astype(BF16)
    d = w_in.shape[0]
    w_router = jnp.zeros((d, LANES), F32)
    w_router = w_router.at[:, :N_EXPERTS].set(p['w_re'][l]).at[:, N_EXPERTS:N_EXPERTS + N_GROUPS].set(p['w_rg'][l])
    b_router = jnp.zeros((1, LANES), F32)
    b_router = b_router.at[0, :N_EXPERTS].set(p['b_re'][l]).at[0, N_EXPERTS:N_EXPERTS + N_GROUPS].set(p['b_rg'][l])
    w_kv = cols((3, 4, 6, 7, 9, 10))
    return dict(
        w_q=cols((0, 1, 2, 5, 8)),
        w_kv=w_kv, w_kv_t=w_kv.T,
        pvec=jnp.concatenate([p['conv_b'][l][None], p['lru_ba'][l][None], p['lru_bx'][l][None],
                              p['lru_lambda'][l][None], p['conv_w'][l]], axis=0),
        w_gates=jnp.concatenate([_block_diag(p['lru_wa'][l]), _block_diag(p['lru_wx'][l])], axis=1).astype(BF16),
        dl=p['diff_lambda'][l],
        subln=jnp.tile(p['diff_subln'][l], N_HEADS)[None],
        lam_init=0.8 - 0.6 * math.exp(-0.3 * l),
        w_out=p['w_out'][l].astype(BF16),
        ln1=jnp.stack([p['ln1_g'][l], p['ln1_b'][l]]),
        w_xq=p['w_xq'][l].astype(BF16),
        w_xo=p['w_xo'][l].astype(BF16),
        ln2=jnp.stack([p['ln2_g'][l], p['ln2_b'][l]]),
        w_router=w_router, b_router=b_router,
        w_gate=p['w_gate'][l].astype(BF16), w_up=p['w_up'][l].astype(BF16), w_down=p['w_down'][l].astype(BF16),
        ln3=jnp.stack([p['ln3_g'][l], p['ln3_b'][l]]),
        w_xkv_t=p['w_xkv'][l].astype(BF16).T,
    )


def _pack_state(conv_buf, h0):
    b, _, w = conv_buf.shape
    pad = jnp.zeros((b, SUBLANES - CONV_WIDTH, w), F32)
    return jnp.concatenate([conv_buf, h0[:, None], pad], axis=1)


def _finish_layer(x, ys, mem_kv_t, layer, lp, alpha, tm):
    b, t, d = x.shape
    x2 = _post(ys, x, lp['w_out'], lp['ln1'], lp['w_xq'], mem_kv_t, layer, lp['w_xo'], lp['ln2'], alpha, tm)
    x3 = _moe(x2.reshape(b * t, d), lp['w_router'], lp['b_router'], lp['w_gate'], lp['w_up'], lp['w_down'],
              lp['ln3'], alpha, ROW_TILE)
    return x3.reshape(b, t, d)


def _token_major(x_t, lead):
    n_slots = x_t.shape[-2] // GROUP_WIDTH
    x = x_t.reshape(x_t.shape[:lead] + (n_slots, N_HEADS, HEAD_DIM, x_t.shape[-1]))
    return jnp.moveaxis(x, -1, lead)


def kernel(x_prompt, x_sample, cache_kv, cache_mem_kv, state_rglru_h, state_conv, page_table, mem_prompt, rel_bias, w_in, conv_w, conv_b, lru_wa, lru_ba, lru_wx, lru_bx, lru_lambda, diff_lambda, diff_subln, w_out, ln1_g, ln1_b, w_xq, w_xkv, w_xo, ln2_g, ln2_b, w_rg, b_rg, w_re, b_re, w_gate, w_up, w_down, ln3_g, ln3_b):
    p = dict(w_in=w_in, conv_w=conv_w, conv_b=conv_b, lru_wa=lru_wa, lru_ba=lru_ba, lru_wx=lru_wx, lru_bx=lru_bx,
             lru_lambda=lru_lambda, diff_lambda=diff_lambda, diff_subln=diff_subln, w_out=w_out, ln1_g=ln1_g,
             ln1_b=ln1_b, w_xq=w_xq, w_xkv=w_xkv, w_xo=w_xo, ln2_g=ln2_g, ln2_b=ln2_b, w_rg=w_rg, b_rg=b_rg,
             w_re=w_re, b_re=b_re, w_gate=w_gate, w_up=w_up, w_down=w_down, ln3_g=ln3_g, ln3_b=ln3_b)
    depth = w_in.shape[0]
    alpha = (2 * depth) ** 0.25
    bp, tp, d = x_prompt.shape
    bs, ts, _ = x_sample.shape
    w = GROUP_WIDTH
    n_pool, _, page = cache_kv.shape[:3]
    assert tp % TQ == 0 and ts <= SAMPLE_ROWS and (page_table.shape[1] * page) % MOBA_BLOCK == 0
    m_len = mem_prompt.shape[1]
    cache_t = jnp.moveaxis(cache_kv, 2, -1).reshape(n_pool, depth, N_KV_SLOTS * w, page)
    mem_sample_t = jnp.moveaxis(cache_mem_kv, 2, -1).reshape(bs, depth, 2 * w, m_len)

    tab_moba, tab_diff = rel_bias[:, :N_HEADS], rel_bias[:, N_HEADS:]
    pb_moba, far_moba = _prompt_bias(tab_moba)
    pb_diff, far_diff = _prompt_bias(tab_diff)
    sb_moba = _sample_bias(tab_moba, page, ts)
    sb_diff = jnp.concatenate([_sample_bias(tab_diff, page, ts)] * 2, axis=1)

    xp = x_prompt
    xs = jnp.pad(x_sample, ((0, 0), (0, SAMPLE_ROWS - ts), (0, 0)))
    st_p = jnp.zeros((bp, SUBLANES, w), F32)
    outs = dict(kv_p=[], mem_p=[], h_p=[], conv_p=[], kv_s=[], h_s=[], conv_s=[])
    for l in range(depth):
        lp = _layer_params(l, p)
        mkv_t = _memory_kv_t(mem_prompt, lp['w_xkv_t'])

        qag, kvt, kvb = _project(xp, lp['w_q'], lp['w_kv_t'])
        y_a, st = _rglru(qag, st_p, lp['pvec'], lp['w_gates'], tc=TQ, tv=TQ)
        y_b, y_c, y_d = _prompt_mixers(qag, kvt, kvb, pb_moba, far_moba, pb_diff, far_diff,
                                       lp['dl'], lp['subln'], lp['lam_init'])
        xp = _finish_layer(xp, (y_a, y_b, y_c, y_d), mkv_t, 0, lp, alpha, ROW_TILE)
        outs['kv_p'].append(kvt)
        outs['mem_p'].append(mkv_t[:, 0])
        outs['h_p'].append(st[:, CONV_WIDTH - 1])
        outs['conv_p'].append(st[:, :CONV_WIDTH - 1])

        qag, kv = _matmul(xs.reshape(bs * SAMPLE_ROWS, d), [lp['w_q'], lp['w_kv']])
        qag, kv = qag.reshape(bs, SAMPLE_ROWS, 5 * w), kv.reshape(bs, SAMPLE_ROWS, N_KV_SLOTS * w)
        y_a, st = _rglru(qag, _pack_state(state_conv[:, l], state_rglru_h[:, l]), lp['pvec'], lp['w_gates'],
                         tc=SAMPLE_ROWS, tv=ts)
        new_page = jnp.pad(jnp.swapaxes(kv, 1, 2), ((0, 0), (0, 0), (0, page - SAMPLE_ROWS)))
        y3 = _paged_attention(qag[:, :, 2 * w:], new_page, cache_t, page_table, l, sb_moba, sb_diff,
                              lp['dl'], lp['subln'], lp['lam_init'])
        xs = _finish_layer(xs, (y_a, y3[:, :, :w], y3[:, :, w:2 * w], y3[:, :, 2 * w:]), mem_sample_t, l,
                           lp, alpha, SAMPLE_ROWS)
        outs['kv_s'].append(kv[:, :ts].reshape(bs, ts, N_KV_SLOTS, N_HEADS, HEAD_DIM))
        outs['h_s'].append(st[:, CONV_WIDTH - 1])
        outs['conv_s'].append(st[:, :CONV_WIDTH - 1])

    stack = lambda k: jnp.stack(outs[k], axis=1)
    return (xp, xs[:, :ts], _token_major(stack('kv_p'), 2), _token_major(stack('mem_p'), 2), stack('h_p'),
            stack('conv_p'), stack('kv_s'), stack('h_s'), stack('conv_s'))
```

```python
import functools
import math

import numpy as np
import jax
import jax.numpy as jnp
from jax import lax
from jax.experimental import pallas as pl
from jax.experimental.pallas import tpu as pltpu

F32 = jnp.float32
BF16 = jnp.bfloat16

HEAD_DIM = 64
N_HEADS = 4
GROUP_WIDTH = N_HEADS * HEAD_DIM
N_KV_SLOTS = 6
CONV_WIDTH = 4
LRU_C = 8.0
MOBA_BLOCK = 256
MOBA_TOPK = 3
DIFF_HALF = HEAD_DIM // 2
N_BUCKETS = 32
MAX_EXACT = N_BUCKETS // 2
MAX_DISTANCE = 128
N_GROUPS = 4
EXPERTS_PER_GROUP = 4
N_EXPERTS = N_GROUPS * EXPERTS_PER_GROUP
LN_EPS = 1e-5
NEG = -1e30
LOG2E = math.log2(math.e)
SOFTPLUS_LINEAR = 64.0
STICK_DEAD = -192.0
LANES = 128
SUBLANES = 8
VMEM_LIMIT = 56 * 1024 * 1024
TQ = MOBA_BLOCK
Q_TILES = 2
EXPERTS_PER_STEP = 2
ROW_TILE = 1024
SAMPLE_ROWS = SUBLANES
PAGES_PER_STEP = 8


def _cparams(*sem):
    return pltpu.CompilerParams(dimension_semantics=sem, vmem_limit_bytes=VMEM_LIMIT)


def _softplus(x):
    return jnp.maximum(x, 0.0) + jnp.log1p(jnp.exp(-jnp.abs(x)))


def _sigmoid(x):
    return 1.0 / (1.0 + jnp.exp(-x))


def _dot(a, b, precision=None):
    return jnp.dot(a, b, preferred_element_type=F32, precision=precision)


def _dot_nt(a, b):
    return lax.dot_general(a, b, (((1,), (1,)), ((), ())), preferred_element_type=F32)


def _layer_norm(x, g, b):
    mu = jnp.mean(x, axis=-1, keepdims=True)
    xc = x - mu
    var = jnp.mean(xc * xc, axis=-1, keepdims=True)
    return xc * lax.rsqrt(var + LN_EPS) * g + b


def _bucket_table(n):
    d = np.arange(n)
    large = MAX_EXACT + (np.log(np.maximum(d, 1).astype(np.float32) / MAX_EXACT)
                         / math.log(MAX_DISTANCE / MAX_EXACT) * (N_BUCKETS - MAX_EXACT)).astype(np.int32)
    return np.where(d < MAX_EXACT, d, np.minimum(large, N_BUCKETS - 1)).astype(np.int32)


def _mm_kernel(*refs, n_out):
    x = refs[0][...].astype(BF16)
    for w_ref, o_ref in zip(refs[1:1 + n_out], refs[1 + n_out:]):
        o_ref[...] = _dot(x, w_ref[...])


def _matmul(x, ws, tm=512):
    m, k = x.shape
    tm = min(tm, m)
    return pl.pallas_call(
        functools.partial(_mm_kernel, n_out=len(ws)),
        grid=(m // tm,),
        in_specs=[pl.BlockSpec((tm, k), lambda i: (i, 0))] + [pl.BlockSpec(w.shape, lambda i: (0, 0)) for w in ws],
        out_specs=[pl.BlockSpec((tm, w.shape[1]), lambda i: (i, 0)) for w in ws],
        out_shape=[jax.ShapeDtypeStruct((m, w.shape[1]), F32) for w in ws],
        compiler_params=_cparams("parallel"),
    )(x, *ws)


def _proj_kernel(x_ref, wq_ref, wkvt_ref, q_ref, kvt_ref, kvb_ref):
    x = x_ref[0].astype(BF16)
    q_ref[0] = _dot(x, wq_ref[...])
    kvt = _dot_nt(wkvt_ref[...], x)
    kvt_ref[0] = kvt
    kvb_ref[0] = kvt.astype(BF16)


def _project(x, w_q, w_kv_t, tm=512):
    b, t, d = x.shape
    tm = min(tm, t)
    nq, nkv = w_q.shape[1], w_kv_t.shape[0]
    return pl.pallas_call(
        _proj_kernel,
        grid=(b, t // tm),
        in_specs=[pl.BlockSpec((1, tm, d), lambda i, j: (i, j, 0)),
                  pl.BlockSpec(w_q.shape, lambda i, j: (0, 0)),
                  pl.BlockSpec(w_kv_t.shape, lambda i, j: (0, 0))],
        out_specs=[pl.BlockSpec((1, tm, nq), lambda i, j: (i, j, 0)),
                   pl.BlockSpec((1, nkv, tm), lambda i, j: (i, 0, j)),
                   pl.BlockSpec((1, nkv, tm), lambda i, j: (i, 0, j))],
        out_shape=[jax.ShapeDtypeStruct((b, t, nq), F32), jax.ShapeDtypeStruct((b, nkv, t), F32),
                   jax.ShapeDtypeStruct((b, nkv, t), BF16)],
        compiler_params=_cparams("parallel", "parallel"),
    )(x, w_q, w_kv_t)


def _mem_kernel(m_ref, w_ref, o_ref):
    o_ref[0, 0] = _dot_nt(w_ref[...], m_ref[0].astype(BF16))


def _memory_kv_t(mem, w_xkv_t):
    b, m, d = mem.shape
    n = w_xkv_t.shape[0]
    return pl.pallas_call(
        _mem_kernel, grid=(b,),
        in_specs=[pl.BlockSpec((1, m, d), lambda i: (i, 0, 0)), pl.BlockSpec(w_xkv_t.shape, lambda i: (0, 0))],
        out_specs=pl.BlockSpec((1, 1, n, m), lambda i: (i, 0, 0, 0)),
        out_shape=jax.ShapeDtypeStruct((b, 1, n, m), F32),
        compiler_params=_cparams("parallel"),
    )(mem, w_xkv_t)


def _rglru_kernel(x_ref, g_ref, st_ref, pv_ref, w_ref, y_ref, so_ref, carry_ref, *, tc, tv):
    c = pl.program_id(1)

    @pl.when(c == 0)
    def _():
        carry_ref[...] = st_ref[0]

    w = GROUP_WIDTH
    x = x_ref[0]
    row = lax.broadcasted_iota(jnp.int32, (tc, w), 0)
    pv = pv_ref[...]
    conv_b, ba, bx, lam = pv[0:1], pv[1:2], pv[2:3], pv[3:4]
    prev = carry_ref[...]
    u = conv_b + pv[4 + CONV_WIDTH - 1:4 + CONV_WIDTH] * x
    for sft in range(1, CONV_WIDTH):
        xs = pltpu.roll(x, sft, 0)
        for i in range(sft):
            xs = jnp.where(row == i, prev[CONV_WIDTH - 1 + i - sft:CONV_WIDTH + i - sft], xs)
        u = u + pv[4 + CONV_WIDTH - 1 - sft:4 + CONV_WIDTH - sft] * xs
    gates = _dot(u.astype(BF16), w_ref[...])
    r = _sigmoid(gates[:, :w] + ba)
    ig = _sigmoid(gates[:, w:] + bx)
    log_a = -LRU_C * r * _softplus(-lam)
    a = jnp.exp(log_a)
    b = jnp.sqrt(-jnp.tanh(log_a) * (jnp.exp(2.0 * log_a) + 1.0)) * ig * u
    s = 1
    while s < tc:
        a_s = pltpu.roll(a, s, 0)
        b_s = pltpu.roll(b, s, 0)
        m = row >= s
        b = jnp.where(m, a * b_s + b, b)
        a = jnp.where(m, a * a_s, a)
        s *= 2
    h = b + a * prev[CONV_WIDTH - 1:CONV_WIDTH]
    gt = g_ref[0]
    gelu = 0.5 * gt * (1.0 + jnp.tanh(math.sqrt(2.0 / math.pi) * (gt + 0.044715 * (gt * gt * gt))))
    y_ref[0] = h * gelu
    carry_ref[0:CONV_WIDTH - 1] = x[tv - (CONV_WIDTH - 1):tv]
    carry_ref[CONV_WIDTH - 1:CONV_WIDTH] = h[tv - 1:tv]

    @pl.when(c == pl.num_programs(1) - 1)
    def _():
        so_ref[0] = carry_ref[...]


def _rglru(qag, state, pvec, w_gates, tc, tv):
    b, t, _ = qag.shape
    w = GROUP_WIDTH
    assert t % tc == 0 and tv >= CONV_WIDTH - 1
    return pl.pallas_call(
        functools.partial(_rglru_kernel, tc=tc, tv=tv),
        grid=(b, t // tc),
        in_specs=[pl.BlockSpec((1, tc, w), lambda i, c: (i, c, 0)),
                  pl.BlockSpec((1, tc, w), lambda i, c: (i, c, 1)),
                  pl.BlockSpec((1, SUBLANES, w), lambda i, c: (i, 0, 0)),
                  pl.BlockSpec((SUBLANES, w), lambda i, c: (0, 0)),
                  pl.BlockSpec((w, 2 * w), lambda i, c: (0, 0))],
        out_specs=[pl.BlockSpec((1, tc, w), lambda i, c: (i, c, 0)),
                   pl.BlockSpec((1, SUBLANES, w), lambda i, c: (i, 0, 0))],
        out_shape=[jax.ShapeDtypeStruct((b, t, w), F32), jax.ShapeDtypeStruct((b, SUBLANES, w), F32)],
        scratch_shapes=[pltpu.VMEM((SUBLANES, w), F32)],
        compiler_params=_cparams("parallel", "arbitrary"),
    )(qag, qag, state, pvec, w_gates)


def _tri_tables(n_super):
    qs, ks = [], []
    for qs_i in range(n_super):
        for ki in range(Q_TILES * qs_i + Q_TILES - 1, -1, -1):
            qs.append(qs_i)
            ks.append(ki)
    return jnp.asarray(np.array(qs, np.int32)), jnp.asarray(np.array(ks, np.int32))


def _wide(x):
    return jnp.concatenate([x, x], axis=1)


def _values_with(vt, h, fill):
    rowh = lax.broadcasted_iota(jnp.int32, vt.shape, 0) // HEAD_DIM
    return jnp.where(rowh == h, vt, jnp.full((), fill, vt.dtype))


def _softmax_update(j, rows, t2, m_cand, vt_aug, m_ref, acc_ref, shift=None):
    m_old = m_ref[j, rows, :]
    m_new = jnp.maximum(m_old, m_cand)
    off = m_new if shift is None else m_new - shift
    p = jnp.exp2(t2 - _wide(off))
    acc_ref[j, rows, :] = jnp.exp2(m_old - m_new) * acc_ref[j, rows, :] + _dot_nt(p.astype(BF16), vt_aug)
    m_ref[j, rows, :] = m_new


def _normalised(acc):
    return acc / pltpu.roll(acc, HEAD_DIM, 1)


def _moba_select(q, km, q_blk):
    blk = lax.broadcasted_iota(jnp.int32, q_blk.shape, 1)
    blkf = blk.astype(F32)
    gate = _dot(q, km, precision=lax.Precision.HIGHEST)
    g = jnp.where(blk < q_blk, gate, -jnp.inf)
    sel = jnp.zeros(q_blk.shape, F32)
    for _ in range(MOBA_TOPK):
        mx = jnp.max(g, axis=1, keepdims=True)
        idx = jnp.min(jnp.where(g == mx, blkf, float(LANES)), axis=1, keepdims=True)
        hit = blkf == idx
        sel = jnp.where(hit, 1.0, sel)
        g = jnp.where(hit, -jnp.inf, g)
    return jnp.where(blk < q_blk, sel, 0.0)


def _diff_lambda(dl, lam_init):
    return (jnp.exp(jnp.sum(dl[0:1] * dl[1:2], axis=1, keepdims=True))
            - jnp.exp(jnp.sum(dl[2:3] * dl[3:4], axis=1, keepdims=True)) + lam_init)


def _mix_kernel(qt_ref, kt_ref, qb_ref, qc_ref, qd_ref, kv_ref,
                ntri_ref, km_ref, bias_b_ref, far_b_ref, bias_d_ref, far_d_ref, dl_ref, sub_ref,
                ob_ref, oc_ref, od_ref,
                qmb_ref, sel_ref, mb_ref, accb_ref, qmc_ref, run_ref, accc_ref, qmd_ref, md_ref, accd_ref,
                live_ref, *, lam_init):
    t = pl.program_id(2)
    qs, ki = qt_ref[t], kt_ref[t]
    off = ki - Q_TILES * qs

    @pl.when(off == Q_TILES - 1)
    def _():
        qb = qb_ref[0]
        qb2 = qb * (HEAD_DIM ** -0.5 * LOG2E)
        qc2 = qc_ref[0] * (HEAD_DIM ** -0.5 * LOG2E)
        qd2 = qd_ref[0] * (DIFF_HALF ** -0.5 * LOG2E)
        q_blk = Q_TILES * qs + lax.broadcasted_iota(jnp.int32, (Q_TILES * TQ, LANES), 0) // TQ
        for h in range(2):
            hs = slice(h * HEAD_DIM, (h + 1) * HEAD_DIM)
            sel_ref[:, h * LANES:(h + 1) * LANES] = _moba_select(qb[:, hs], km_ref[0, hs, :], q_blk).astype(BF16)
            qmb_ref[h] = qb2[:, hs].astype(BF16)
            qmc_ref[h] = qc2[:, hs].astype(BF16)
        for j in range(4):
            qmd_ref[j] = qd2[:, j * DIFF_HALF:(j + 1) * DIFF_HALF].astype(BF16)
        mb_ref[...] = jnp.full_like(mb_ref, NEG)
        md_ref[...] = jnp.full_like(md_ref, NEG)
        accb_ref[...] = jnp.zeros_like(accb_ref)
        accc_ref[...] = jnp.zeros_like(accc_ref)
        accd_ref[...] = jnp.zeros_like(accd_ref)
        run_ref[...] = jnp.zeros_like(run_ref)
        live_ref[0] = 1


    def moba_step(rows, dist):
        kt = kv_ref[0, 0, 0]
        vt = kv_ref[0, 1, 0]
        if dist != 0:
            hit = lax.broadcasted_iota(jnp.int32, (LANES, LANES), 0) == ki
            e = jnp.where(hit, 1.0, 0.0).astype(BF16)
            z = jnp.zeros_like(e)
            onehot = jnp.concatenate([jnp.concatenate([e, z], axis=1), jnp.concatenate([z, e], axis=1)], axis=0)
            picked = _dot(sel_ref[rows, :], onehot)
        for h in range(2):
            raw = _dot(qmb_ref[h, rows, :], kt[h * HEAD_DIM:(h + 1) * HEAD_DIM])
            vt_aug = _values_with(vt, h, 1.0)
            row_pen = None if dist == 0 else jnp.where(picked[:, h * LANES:(h + 1) * LANES] > 0.5, 0.0, NEG)
            if dist is not None:
                t2 = raw + bias_b_ref[h, dist]
                m_cand = jnp.max(t2, axis=1, keepdims=True)
                m_cand = m_cand if dist == 0 else m_cand + row_pen
                _softmax_update(h, rows, t2, m_cand, vt_aug, mb_ref, accb_ref, shift=row_pen)
            else:
                shift = far_b_ref[h, 0:1] + row_pen
                m_cand = jnp.max(raw, axis=1, keepdims=True) + shift
                _softmax_update(h, rows, raw, m_cand, vt_aug, mb_ref, accb_ref, shift=shift)

    def stick_step(rows, dist):
        kt = kv_ref[0, 2, 0]
        vt = kv_ref[0, 3, 0]
        n = rows.stop - rows.start
        if dist == 0:
            r = lax.broadcasted_iota(jnp.int32, (TQ, TQ), 0)
            c = lax.broadcasted_iota(jnp.int32, (TQ, TQ), 1)
            valid = c < r
        z2s, runs, parts = [], [], []
        for h in range(2):
            z2 = _dot(qmc_ref[h, rows, :], kt[h * HEAD_DIM:(h + 1) * HEAD_DIM])
            sp2 = jnp.where(z2 > SOFTPLUS_LINEAR, z2, jnp.log2(1.0 + jnp.exp2(z2)))
            if dist == 0:
                sp2 = jnp.where(valid, sp2, 0.0)
            z2s.append(z2)
            parts.append(sp2.astype(BF16))
            runs.append(run_ref[h, rows, :] - jnp.broadcast_to(jnp.sum(sp2, axis=1, keepdims=True), (n, LANES)))
        cums = _dot(jnp.concatenate(parts, axis=0), ntri_ref[...])
        acc = accc_ref[rows, :]
        for h in range(2):
            e = jnp.exp2(z2s[h] + cums[h * n:(h + 1) * n] + _wide(run_ref[h, rows, :]))
            if dist == 0:
                e = jnp.where(valid, e, 0.0)
            acc = acc + _dot_nt(e.astype(BF16), _values_with(vt, h, 0.0))
            run_ref[h, rows, :] = runs[h]
        accc_ref[rows, :] = acc

    def diff_step(rows, dist):
        kt = kv_ref[0, 4, 0]
        vt = kv_ref[0, 5, 0]
        vt_aug = [_values_with(vt, h, 1.0) for h in range(2)]
        for j in range(4):
            h = j // 2
            raw = _dot(qmd_ref[j, rows, :], kt[j * DIFF_HALF:(j + 1) * DIFF_HALF])
            if dist is not None:
                t2 = raw + bias_d_ref[h, dist]
                _softmax_update(j, rows, t2, jnp.max(t2, axis=1, keepdims=True), vt_aug[h], md_ref, accd_ref)
            else:
                shift = far_d_ref[h, 0:1]
                m_cand = jnp.max(raw, axis=1, keepdims=True) + shift
                _softmax_update(j, rows, raw, m_cand, vt_aug[h], md_ref, accd_ref, shift=shift)

    def all_mixers(rows, dist):
        moba_step(rows, dist)
        stick_step(rows, dist)
        diff_step(rows, dist)

    def near_diagonal_step(o):
        for r in range(max(o, 0), min(o + 2, Q_TILES)):
            all_mixers(slice(r * TQ, (r + 1) * TQ), r - o)
        if o + 2 < Q_TILES:
            all_mixers(slice((o + 2) * TQ, Q_TILES * TQ), None)

    for o in range(Q_TILES - 1, -2, -1):
        pl.when(off == o)(functools.partial(near_diagonal_step, o))

    @pl.when(off <= -2)
    def _():
        rows = slice(0, Q_TILES * TQ)
        moba_step(rows, None)
        diff_step(rows, None)

        @pl.when(live_ref[0] == 1)
        def _():
            stick_step(rows, None)
            live_ref[0] = (jnp.max(run_ref[...]) >= STICK_DEAD).astype(jnp.int32)

    @pl.when(ki == 0)
    def _():
        head = lax.broadcasted_iota(jnp.int32, (Q_TILES * TQ, LANES), 1) // HEAD_DIM
        ob_ref[0] = jnp.where(head == 0, _normalised(accb_ref[0]), _normalised(accb_ref[1]))
        oc_ref[0] = accc_ref[...]
        lam = _diff_lambda(dl_ref[...], lam_init)
        o = jnp.where(head == 0,
                      _normalised(accd_ref[0]) - lam * _normalised(accd_ref[1]),
                      _normalised(accd_ref[2]) - lam * _normalised(accd_ref[3]))
        o2 = o * o
        ms = jnp.where(head == 0,
                       jnp.sum(jnp.where(head == 0, o2, 0.0), axis=1, keepdims=True),
                       jnp.sum(jnp.where(head == 1, o2, 0.0), axis=1, keepdims=True)) * (1.0 / HEAD_DIM)
        od_ref[0] = o * lax.rsqrt(ms + LN_EPS) * sub_ref[...] * (1.0 - lam_init)


def _kmean_kernel(k_ref, ones_ref, o_ref):
    o_ref[0] = _dot(k_ref[0], ones_ref[...], precision=lax.Precision.HIGHEST)


def _suffix_tri(n, sign=1.0):
    i = np.arange(n)
    return jnp.asarray(sign * (i[:, None] >= i[None, :]).astype(np.float32), dtype=BF16)


def _prompt_mixers(qag, kvt, kvb, bias_b, far_b, bias_d, far_d, dl, subln, lam_init):
    b, t, _ = qag.shape
    kv5 = kvb.reshape(b, N_KV_SLOTS, 2, LANES, t)
    nb = t // MOBA_BLOCK
    assert nb <= LANES
    blk_mean = np.zeros((t, LANES), np.float32)
    blk_mean[np.arange(t), np.arange(t) // MOBA_BLOCK] = 1.0 / MOBA_BLOCK
    kmean_t = pl.pallas_call(
        _kmean_kernel, grid=(b,),
        in_specs=[pl.BlockSpec((1, GROUP_WIDTH, t), lambda i: (i, 0, 0)),
                  pl.BlockSpec((t, LANES), lambda i: (0, 0))],
        out_specs=pl.BlockSpec((1, GROUP_WIDTH, LANES), lambda i: (i, 0, 0)),
        out_shape=jax.ShapeDtypeStruct((b, GROUP_WIDTH, LANES), F32),
        compiler_params=_cparams("parallel"),
    )(kvt, jnp.asarray(blk_mean))
    rows = Q_TILES * TQ
    assert t % rows == 0
    qt, kt = _tri_tables(t // rows)
    q_spec = lambda col: pl.BlockSpec((1, rows, LANES), lambda b, p, t, qt, kt: (b, qt[t], col + p))
    kv_spec = pl.BlockSpec((1, N_KV_SLOTS, 1, LANES, TQ), lambda b, p, t, qt, kt: (b, 0, p, 0, kt[t]))
    bias_spec = pl.BlockSpec((2, 2, TQ, TQ), lambda b, p, t, qt, kt: (p, 0, 0, 0))
    far_spec = pl.BlockSpec((2, SUBLANES, LANES), lambda b, p, t, qt, kt: (p, 0, 0))
    const2 = lambda shape: pl.BlockSpec(shape, lambda b, p, t, qt, kt: (0, 0))
    out_spec = pl.BlockSpec((1, rows, LANES), lambda b, p, t, qt, kt: (b, qt[t], p))
    out = jax.ShapeDtypeStruct((b, t, GROUP_WIDTH), F32)
    stats = lambda n: pltpu.VMEM((n, rows, LANES), F32)
    return pl.pallas_call(
        functools.partial(_mix_kernel, lam_init=lam_init),
        grid_spec=pltpu.PrefetchScalarGridSpec(
            num_scalar_prefetch=2, grid=(b, 2, qt.shape[0]),
            in_specs=[q_spec(4), q_spec(6), q_spec(8), kv_spec]
                     + [const2((TQ, TQ)), pl.BlockSpec((1, LANES, LANES), lambda b, p, t, qt, kt: (b, p, 0)),
                        bias_spec, far_spec, bias_spec, far_spec, const2(dl.shape), const2((1, LANES))],
            out_specs=[out_spec] * 3,
            scratch_shapes=[pltpu.VMEM((2, rows, HEAD_DIM), BF16), pltpu.VMEM((rows, 2 * LANES), BF16),
                            stats(2), stats(2),
                            pltpu.VMEM((2, rows, HEAD_DIM), BF16), stats(2), pltpu.VMEM((rows, LANES), F32),
                            pltpu.VMEM((4, rows, DIFF_HALF), BF16), stats(4), stats(4),
                            pltpu.SMEM((1,), jnp.int32)]),
        out_shape=[out] * 3,
        compiler_params=_cparams("parallel", "parallel", "arbitrary"),
    )(qt, kt, qag, qag, qag, kv5, _suffix_tri(TQ, -1.0), kmean_t,
      bias_b, far_b, bias_d, far_d, dl, subln[:, :LANES])


def _toeplitz(g):
    h, l = g.shape
    n = l // 2
    w = jnp.concatenate([g[:, :1], jnp.flip(g[:, 1:], axis=1)], axis=1)
    rep = jnp.tile(w, (1, n))[:, :n * (l - 1)].reshape(h, n, l - 1)
    return rep[:, :, :n]


def _prompt_bias(tab):
    n = TQ
    bucket = _bucket_table(4 * n)
    assert (bucket[n + 1:] == bucket[-1]).all()
    x = np.arange(2 * n)
    d = np.where(x < n, x, x - 2 * n)
    fvec = (tab * LOG2E).T
    g0 = jnp.where(jnp.asarray(d >= 0), fvec[:, bucket[np.maximum(d, 0)]], NEG)
    g1 = fvec[:, bucket[n + d]]
    tiles = jnp.stack([_toeplitz(g0), _toeplitz(g1)], axis=1)
    far = jnp.broadcast_to(fvec[:, bucket[-1]][:, None, None], (tab.shape[1], SUBLANES, LANES))
    return tiles, far


def _expand_rows(x4):
    x = jnp.concatenate([x4] * N_HEADS, axis=0)
    rh = lax.broadcasted_iota(jnp.int32, x.shape, 0) // SAMPLE_ROWS
    lh = lax.broadcasted_iota(jnp.int32, x.shape, 1) // HEAD_DIM
    return jnp.where(rh == lh, x, 0.0)


def _collapse_rows(x):
    rh = lax.broadcasted_iota(jnp.int32, x.shape, 0) // SAMPLE_ROWS
    lh = lax.broadcasted_iota(jnp.int32, x.shape, 1) // HEAD_DIM
    x = jnp.where(rh == lh, x, 0.0)
    out = x[0:SAMPLE_ROWS]
    for h in range(1, N_HEADS):
        out = out + x[h * SAMPLE_ROWS:(h + 1) * SAMPLE_ROWS]
    return out


def _paged_kernel(pt_ref, q_ref, new_ref, *rest, g_pages, lam_init):
    page_refs = rest[:g_pages]
    bm_ref, bd_ref, tri_ref, dl_ref, sub_ref, o_ref = rest[g_pages:g_pages + 6]
    (qb_ref, qc_ref, qd_ref, mb_m, mb_l, mb_g, mb_acc,
     run_ref, accc_ref, md_ref, ld_ref, accd_ref, live_ref) = rest[g_pages + 6:]
    s_idx = pl.program_id(1)
    n_steps = pl.num_programs(1)
    nblk = mb_m.shape[0] - 1
    w = GROUP_WIDTH
    rows = N_HEADS * SAMPLE_ROWS
    page = new_ref.shape[2]
    qq = lax.broadcasted_iota(jnp.int32, (rows, page), 0) % SAMPLE_ROWS
    kk = lax.broadcasted_iota(jnp.int32, (rows, page), 1)
    tri = tri_ref[...]

    def slot(pg, i):
        return pg[i * w:(i + 1) * w, :].astype(BF16)

    def moba_block(pages, biases, pens, n):
        ss, gsum = [], None
        for pg, bias, pen in zip(pages, biases, pens):
            raw = _dot(qb_ref[...], slot(pg, 0))
            sc = raw + bias
            ss.append(sc if pen is None else sc + pen)
            gsum = raw if gsum is None else gsum + raw
        m = ss[0].max(axis=1, keepdims=True)
        for sc in ss[1:]:
            m = jnp.maximum(m, sc.max(axis=1, keepdims=True))
        psum, acc = None, None
        for pg, sc in zip(pages, ss):
            p = jnp.exp(sc - m)
            pa = _dot_nt(p.astype(BF16), slot(pg, 1))
            psum = p if psum is None else psum + p
            acc = pa if acc is None else acc + pa
        mb_m[n] = m
        mb_l[n] = jnp.sum(psum, axis=1, keepdims=True)
        mb_acc[n] = acc
        mb_g[n] = jnp.sum(gsum, axis=1, keepdims=True)

    def stick_page(pg, valid):
        z = _dot(qc_ref[...], slot(pg, 2))
        lk = -_softplus(z)
        if valid is not None:
            lk = jnp.where(valid, lk, 0.0)
        hi = lk.astype(BF16)
        lo = (lk - hi.astype(F32)).astype(BF16)
        cum = _dot(hi, tri) + _dot(lo, tri)
        e = jnp.exp(z + cum + run_ref[...])
        if valid is not None:
            e = jnp.where(valid, e, 0.0)
        accc_ref[...] += _dot_nt(e.astype(BF16), slot(pg, 3))
        run_ref[...] += jnp.sum(lk, axis=1, keepdims=True)

    def diff_pages(pages, biases, pens):
        ss = []
        for pg, bias, pen in zip(pages, biases, pens):
            sc = _dot(qd_ref[...], slot(pg, 4)) * DIFF_HALF ** -0.5 + bias
            ss.append(sc if pen is None else sc + pen)
        mx = ss[0]
        for sc in ss[1:]:
            mx = jnp.maximum(mx, sc)
        m_old = md_ref[...]
        m_new = jnp.maximum(m_old, mx.max(axis=1, keepdims=True))
        alpha = jnp.exp(m_old - m_new)
        psum, acc = None, None
        for pg, sc in zip(pages, ss):
            p = jnp.exp(sc - m_new)
            pa = _dot_nt(p.astype(BF16), slot(pg, 5))
            psum = p if psum is None else psum + p
            acc = pa if acc is None else acc + pa
        ld_ref[...] = alpha * ld_ref[...] + jnp.sum(psum, axis=1, keepdims=True)
        accd_ref[...] = alpha * accd_ref[...] + acc
        md_ref[...] = m_new

    @pl.when(s_idx == 0)
    def _():
        q = q_ref[0]
        qb_ref[...] = (_expand_rows(q[:, 0:w]) * HEAD_DIM ** -0.5).astype(BF16)
        qc_ref[...] = (_expand_rows(q[:, w:2 * w]) * HEAD_DIM ** -0.5).astype(BF16)
        qdx = _expand_rows(q[:, 2 * w:3 * w])
        half = lax.broadcasted_iota(jnp.int32, qdx.shape, 1) // DIFF_HALF % 2
        qd_ref[0:rows] = jnp.where(half == 0, qdx, 0.0).astype(BF16)
        qd_ref[rows:2 * rows] = jnp.where(half == 1, qdx, 0.0).astype(BF16)
        run_ref[...] = jnp.zeros_like(run_ref)
        live_ref[0] = 1
        accc_ref[...] = jnp.zeros_like(accc_ref)
        md_ref[...] = jnp.full_like(md_ref, NEG)
        ld_ref[...] = jnp.zeros_like(ld_ref)
        accd_ref[...] = jnp.zeros_like(accd_ref)
        new = new_ref[0]
        causal_pen = jnp.where(kk <= qq, 0.0, NEG)
        moba_block([new], [bm_ref[0]], [causal_pen], nblk)
        stick_page(new, kk < qq)
        diff_pages([new], [bd_ref[0]], [jnp.concatenate([causal_pen, causal_pen], axis=0)])

    pages = [r[0, 0] for r in page_refs]
    bm0 = jnp.where(s_idx == 0, bm_ref[1], bm_ref[2])
    bd0 = jnp.where(s_idx == 0, bd_ref[1], bd_ref[2])
    bms = [bm0] + [bm_ref[2]] * (g_pages - 1)
    bds = [bd0] + [bd_ref[2]] * (g_pages - 1)
    per_block = MOBA_BLOCK // page
    for i in range(0, g_pages, per_block):
        n = nblk - 1 - (s_idx * g_pages + i) // per_block
        moba_block(pages[i:i + per_block], bms[i:i + per_block], [None] * per_block, n)
    diff_pages(pages, bds, [None] * g_pages)

    @pl.when(live_ref[0] == 1)
    def _():
        for pg in pages:
            stick_page(pg, None)
        live_ref[0] = (jnp.max(run_ref[...]) >= STICK_DEAD / LOG2E).astype(jnp.int32)

    @pl.when(s_idx == n_steps - 1)
    def _():
        blkf = lax.broadcasted_iota(jnp.int32, (nblk, rows, 1), 0).astype(F32)
        g = mb_g[0:nblk]
        sel = jnp.zeros((nblk, rows, 1), F32)
        for _ in range(min(MOBA_TOPK, nblk)):
            mx = jnp.max(g, axis=0, keepdims=True)
            idx = jnp.min(jnp.where(g == mx, blkf, float(nblk)), axis=0, keepdims=True)
            hit = blkf == idx
            sel = jnp.where(hit, 1.0, sel)
            g = jnp.where(hit, -jnp.inf, g)
        m_own = mb_m[nblk]
        m_tot = jnp.maximum(m_own, jnp.max(jnp.where(sel > 0.5, mb_m[0:nblk], NEG), axis=0))
        wgt = jnp.where(sel > 0.5, jnp.exp(mb_m[0:nblk] - m_tot), 0.0)
        w_own = jnp.exp(m_own - m_tot)
        l_tot = w_own * mb_l[nblk] + jnp.sum(wgt * mb_l[0:nblk], axis=0)
        a_tot = w_own * mb_acc[nblk] + jnp.sum(wgt * mb_acc[0:nblk], axis=0)
        o_ref[0, :, 0:w] = _collapse_rows(a_tot / l_tot)
        o_ref[0, :, w:2 * w] = _collapse_rows(accc_ref[...])
        lam = _diff_lambda(dl_ref[...], lam_init)
        od = (accd_ref[0:rows] / ld_ref[0:rows] - lam * (accd_ref[rows:2 * rows] / ld_ref[rows:2 * rows]))
        od = _collapse_rows(od)
        od2 = od * od
        lh = lax.broadcasted_iota(jnp.int32, od.shape, 1) // HEAD_DIM
        ms = jnp.zeros_like(od)
        for h in range(N_HEADS):
            ms = jnp.where(lh == h, jnp.sum(jnp.where(lh == h, od2, 0.0), axis=1, keepdims=True), ms)
        o_ref[0, :, 2 * w:3 * w] = od * lax.rsqrt(ms * (1.0 / HEAD_DIM) + LN_EPS) * sub_ref[...] * (1.0 - lam_init)


def _paged_attention(q3, new_page, cache_t, page_table, layer, bm, bd, dl, subln, lam_init):
    b = q3.shape[0]
    n_pages = page_table.shape[1]
    page = cache_t.shape[3]
    g_pages = min(PAGES_PER_STEP, n_pages)
    assert n_pages % g_pages == 0 and MOBA_BLOCK % page == 0 and g_pages % (MOBA_BLOCK // page) == 0
    nblk = n_pages * page // MOBA_BLOCK
    rows = N_HEADS * SAMPLE_ROWS
    w = GROUP_WIDTH

    def page_spec(i):
        return pl.BlockSpec((1, 1, N_KV_SLOTS * w, page),
                            lambda bi, s, pt: (pt[bi, n_pages - 1 - (s * g_pages + i)], layer, 0, 0))

    const2 = lambda bi, s, pt: (0, 0)
    const3 = lambda bi, s, pt: (0, 0, 0)
    return pl.pallas_call(
        functools.partial(_paged_kernel, g_pages=g_pages, lam_init=lam_init),
        grid_spec=pltpu.PrefetchScalarGridSpec(
            num_scalar_prefetch=1, grid=(b, n_pages // g_pages),
            in_specs=[pl.BlockSpec((1, SAMPLE_ROWS, 3 * w), lambda bi, s, pt: (bi, 0, 0)),
                      pl.BlockSpec((1, N_KV_SLOTS * w, page), lambda bi, s, pt: (bi, 0, 0))]
                     + [page_spec(i) for i in range(g_pages)]
                     + [pl.BlockSpec(bm.shape, const3), pl.BlockSpec(bd.shape, const3),
                        pl.BlockSpec((page, page), const2), pl.BlockSpec(dl.shape, const2),
                        pl.BlockSpec((1, w), const2)],
            out_specs=pl.BlockSpec((1, SAMPLE_ROWS, 3 * w), lambda bi, s, pt: (bi, 0, 0)),
            scratch_shapes=[pltpu.VMEM((rows, w), BF16), pltpu.VMEM((rows, w), BF16),
                            pltpu.VMEM((2 * rows, w), BF16),
                            pltpu.VMEM((nblk + 1, rows, 1), F32), pltpu.VMEM((nblk + 1, rows, 1), F32),
                            pltpu.VMEM((nblk + 1, rows, 1), F32), pltpu.VMEM((nblk + 1, rows, w), F32),
                            pltpu.VMEM((rows, 1), F32), pltpu.VMEM((rows, w), F32),
                            pltpu.VMEM((2 * rows, 1), F32), pltpu.VMEM((2 * rows, 1), F32),
                            pltpu.VMEM((2 * rows, w), F32), pltpu.SMEM((1,), jnp.int32)]),
        out_shape=jax.ShapeDtypeStruct((b, SAMPLE_ROWS, 3 * w), F32),
        compiler_params=_cparams("parallel", "arbitrary"),
    )(page_table, q3, new_page, *([cache_t] * g_pages), bm, bd, _suffix_tri(page), dl, subln)


def _sample_bias(tab, page, ts):
    bucket = _bucket_table(2 * page + SAMPLE_ROWS)
    assert (bucket[page + 1:] == bucket[-1]).all()
    q = np.minimum(np.arange(SAMPLE_ROWS), ts - 1)
    k = np.arange(page)
    idx = np.stack([bucket[np.maximum(q[:, None] - k[None, :], 0)],
                    bucket[page + q[:, None] - k[None, :]],
                    np.full((SAMPLE_ROWS, page), bucket[-1])])
    b = jnp.transpose(tab[jnp.asarray(idx)], (0, 3, 1, 2))
    return b.reshape(3, N_HEADS * SAMPLE_ROWS, page)


def _post_kernel(ya_ref, yb_ref, yc_ref, yd_ref, x_ref, wo_ref, ln1_ref, wq_ref, mem_ref, wxo_ref, ln2_ref,
                 o_ref, *, alpha):
    w = GROUP_WIDTH
    mixed = None
    for i, y_ref in enumerate((ya_ref, yb_ref, yc_ref, yd_ref)):
        part = _dot(y_ref[0].astype(BF16), wo_ref[i * w:(i + 1) * w, :])
        mixed = part if mixed is None else mixed + part
    x1 = _layer_norm(alpha * x_ref[0] + mixed, ln1_ref[0:1], ln1_ref[1:2])
    qx = _dot(x1.astype(BF16), wq_ref[...])
    kt = mem_ref[0, 0, :w, :].astype(BF16)
    vt = mem_ref[0, 0, w:, :].astype(BF16)
    head = lax.broadcasted_iota(jnp.int32, qx.shape, 1) // HEAD_DIM
    o = jnp.zeros_like(qx)
    for h in range(N_HEADS):
        s = _dot(jnp.where(head == h, qx, 0.0).astype(BF16), kt) * HEAD_DIM ** -0.5
        p = jnp.exp(s - jnp.max(s, axis=1, keepdims=True))
        p = p / jnp.sum(p, axis=1, keepdims=True)
        o = jnp.where(head == h, _dot_nt(p.astype(BF16), vt), o)
    xo = _dot(o.astype(BF16), wxo_ref[...])
    o_ref[0] = _layer_norm(alpha * x1 + xo, ln2_ref[0:1], ln2_ref[1:2])


def _post(ys, x, w_out, ln1, w_xq, mem_kv_t, layer, w_xo, ln2, alpha, tm):
    b, t, d = x.shape
    w = GROUP_WIDTH
    tm = min(tm, t)
    m_len = mem_kv_t.shape[3]
    row = lambda width: pl.BlockSpec((1, tm, width), lambda i, j: (i, j, 0))
    const = lambda shape: pl.BlockSpec(shape, lambda i, j: (0,) * len(shape))
    return pl.pallas_call(
        functools.partial(_post_kernel, alpha=alpha),
        grid=(b, t // tm),
        in_specs=[row(w)] * 4 + [row(d), const(w_out.shape), const(ln1.shape), const(w_xq.shape),
                                 pl.BlockSpec((1, 1, 2 * w, m_len), lambda i, j: (i, layer, 0, 0)),
                                 const(w_xo.shape), const(ln2.shape)],
        out_specs=row(d),
        out_shape=jax.ShapeDtypeStruct((b, t, d), F32),
        compiler_params=_cparams("parallel", "parallel"),
    )(*ys, x, w_out, ln1, w_xq, mem_kv_t, w_xo, ln2)


def _route(x, wr_ref, br_ref):
    logits = _dot(x, wr_ref[...]) + br_ref[...]
    lane = lax.broadcasted_iota(jnp.int32, logits.shape, 1)
    lanef = lane.astype(F32)
    is_grp = jnp.logical_and(lane >= N_EXPERTS, lane < N_EXPERTS + N_GROUPS)
    lg = jnp.where(is_grp, logits, -jnp.inf)
    gmax = jnp.max(lg, axis=1, keepdims=True)
    pg_sel = 1.0 / jnp.sum(jnp.exp(lg - gmax), axis=1, keepdims=True)
    gsel = jnp.min(jnp.where(lg == gmax, lanef, float(LANES)), axis=1, keepdims=True) - N_EXPERTS
    in_grp = jnp.logical_and(lane < N_EXPERTS, (lane // EXPERTS_PER_GROUP).astype(F32) == gsel)
    le = jnp.where(in_grp, logits, -jnp.inf)
    pe = jnp.exp(le - jnp.max(le, axis=1, keepdims=True))
    pe = pe / jnp.sum(pe, axis=1, keepdims=True)
    cand = jnp.where(in_grp, pe, -1.0)
    p1 = jnp.max(cand, axis=1, keepdims=True)
    i1 = jnp.min(jnp.where(cand == p1, lanef, float(LANES)), axis=1, keepdims=True)
    cand = jnp.where(lanef == i1, -1.0, cand)
    p2 = jnp.max(cand, axis=1, keepdims=True)
    i2 = jnp.min(jnp.where(cand == p2, lanef, float(LANES)), axis=1, keepdims=True)
    top = jnp.where(lanef == i1, p1, 0.0) + jnp.where(lanef == i2, p2, 0.0)
    return pg_sel * top / (p1 + p2)


def _moe_kernel(x_ref, wr_ref, br_ref, wg_ref, wu_ref, wd_ref, ln_ref, o_ref, xb_ref, cw_ref, acc_ref, *, alpha):
    e = pl.program_id(1)

    @pl.when(e == 0)
    def _():
        x = x_ref[...]
        xb_ref[...] = x.astype(BF16)
        cw_ref[...] = _route(xb_ref[...], wr_ref, br_ref)
        acc_ref[...] = jnp.zeros_like(acc_ref)

    xb = xb_ref[...]
    lane = lax.broadcasted_iota(jnp.int32, cw_ref.shape, 1)
    acc = acc_ref[...]
    for k in range(EXPERTS_PER_STEP):
        gate = _dot(xb, wg_ref[k])
        hid = gate * _sigmoid(gate) * _dot(xb, wu_ref[k])
        y = _dot(hid.astype(BF16), wd_ref[k])
        cw = jnp.sum(jnp.where(lane == e * EXPERTS_PER_STEP + k, cw_ref[...], 0.0), axis=1, keepdims=True)
        acc = acc + cw * y
    acc_ref[...] = acc

    @pl.when(e == pl.num_programs(1) - 1)
    def _():
        o_ref[...] = _layer_norm(alpha * x_ref[...] + acc_ref[...], ln_ref[0:1], ln_ref[1:2])


def _moe(x, w_router, b_router, w_gate, w_up, w_down, ln3, alpha, tm):
    n, d = x.shape
    tm = min(tm, n)
    n_e, _, d_ff = w_gate.shape
    return pl.pallas_call(
        functools.partial(_moe_kernel, alpha=alpha),
        grid=(n // tm, n_e // EXPERTS_PER_STEP),
        in_specs=[pl.BlockSpec((tm, d), lambda i, e: (i, 0)),
                  pl.BlockSpec(w_router.shape, lambda i, e: (0, 0)),
                  pl.BlockSpec(b_router.shape, lambda i, e: (0, 0)),
                  pl.BlockSpec((EXPERTS_PER_STEP, d, d_ff), lambda i, e: (e, 0, 0)),
                  pl.BlockSpec((EXPERTS_PER_STEP, d, d_ff), lambda i, e: (e, 0, 0)),
                  pl.BlockSpec((EXPERTS_PER_STEP, d_ff, d), lambda i, e: (e, 0, 0)),
                  pl.BlockSpec(ln3.shape, lambda i, e: (0, 0))],
        out_specs=pl.BlockSpec((tm, d), lambda i, e: (i, 0)),
        out_shape=jax.ShapeDtypeStruct((n, d), F32),
        scratch_shapes=[pltpu.VMEM((tm, d), BF16), pltpu.VMEM((tm, LANES), F32), pltpu.VMEM((tm, d), F32)],
        compiler_params=_cparams("parallel", "arbitrary"),
    )(x, w_router, b_router, w_gate, w_up, w_down, ln3)


def _block_diag(wh):
    h, d, _ = wh.shape
    eye = jnp.eye(h, dtype=wh.dtype)
    return (eye[:, None, :, None] * wh[:, :, None, :]).reshape(h * d, h * d)


def _layer_params(l, p):
    w = GROUP_WIDTH
    w_in = p['w_in'][l]
    cols = lambda idx: jnp.concatenate([w_in[:, i * w:(i + 1) * w] for i in idx], axis=1).astype(BF16)
    d = w_in.shape[0]
    w_router = jnp.zeros((d, LANES), F32)
    w_router = w_router.at[:, :N_EXPERTS].set(p['w_re'][l]).at[:, N_EXPERTS:N_EXPERTS + N_GROUPS].set(p['w_rg'][l])
    b_router = jnp.zeros((1, LANES), F32)
    b_router = b_router.at[0, :N_EXPERTS].set(p['b_re'][l]).at[0, N_EXPERTS:N_EXPERTS + N_GROUPS].set(p['b_rg'][l])
    w_kv = cols((3, 4, 6, 7, 9, 10))
    return dict(
        w_q=cols((0, 1, 2, 5, 8)),
        w_kv=w_kv, w_kv_t=w_kv.T,
        pvec=jnp.concatenate([p['conv_b'][l][None], p['lru_ba'][l][None], p['lru_bx'][l][None],
                              p['lru_lambda'][l][None], p['conv_w'][l]], axis=0),
        w_gates=jnp.concatenate([_block_diag(p['lru_wa'][l]), _block_diag(p['lru_wx'][l])], axis=1).astype(BF16),
        dl=p['diff_lambda'][l],
        subln=jnp.tile(p['diff_subln'][l], N_HEADS)[None],
        lam_init=0.8 - 0.6 * math.exp(-0.3 * l),
        w_out=p['w_out'][l].astype(BF16),
        ln1=jnp.stack([p['ln1_g'][l], p['ln1_b'][l]]),
        w_xq=p['w_xq'][l].astype(BF16),
        w_xo=p['w_xo'][l].astype(BF16),
        ln2=jnp.stack([p['ln2_g'][l], p['ln2_b'][l]]),
        w_router=w_router.astype(BF16), b_router=b_router,
        w_gate=p['w_gate'][l].astype(BF16), w_up=p['w_up'][l].astype(BF16), w_down=p['w_down'][l].astype(BF16),
        ln3=jnp.stack([p['ln3_g'][l], p['ln3_b'][l]]),
        w_xkv_t=p['w_xkv'][l].astype(BF16).T,
    )


def _pack_state(conv_buf, h0):
    b, _, w = conv_buf.shape
    pad = jnp.zeros((b, SUBLANES - CONV_WIDTH, w), F32)
    return jnp.concatenate([conv_buf, h0[:, None], pad], axis=1)


def _finish_layer(x, ys, mem_kv_t, layer, lp, alpha, tm):
    b, t, d = x.shape
    x2 = _post(ys, x, lp['w_out'], lp['ln1'], lp['w_xq'], mem_kv_t, layer, lp['w_xo'], lp['ln2'], alpha, tm)
    x3 = _moe(x2.reshape(b * t, d), lp['w_router'], lp['b_router'], lp['w_gate'], lp['w_up'], lp['w_down'],
              lp['ln3'], alpha, ROW_TILE)
    return x3.reshape(b, t, d)


def _token_major(x_t, lead):
    n_slots = x_t.shape[-2] // GROUP_WIDTH
    x = x_t.reshape(x_t.shape[:lead] + (n_slots, N_HEADS, HEAD_DIM, x_t.shape[-1]))
    return jnp.moveaxis(x, -1, lead)


def kernel(x_prompt, x_sample, cache_kv, cache_mem_kv, state_rglru_h, state_conv, page_table, mem_prompt, rel_bias, w_in, conv_w, conv_b, lru_wa, lru_ba, lru_wx, lru_bx, lru_lambda, diff_lambda, diff_subln, w_out, ln1_g, ln1_b, w_xq, w_xkv, w_xo, ln2_g, ln2_b, w_rg, b_rg, w_re, b_re, w_gate, w_up, w_down, ln3_g, ln3_b):
    p = dict(w_in=w_in, conv_w=conv_w, conv_b=conv_b, lru_wa=lru_wa, lru_ba=lru_ba, lru_wx=lru_wx, lru_bx=lru_bx,
             lru_lambda=lru_lambda, diff_lambda=diff_lambda, diff_subln=diff_subln, w_out=w_out, ln1_g=ln1_g,
             ln1_b=ln1_b, w_xq=w_xq, w_xkv=w_xkv, w_xo=w_xo, ln2_g=ln2_g, ln2_b=ln2_b, w_rg=w_rg, b_rg=b_rg,
             w_re=w_re, b_re=b_re, w_gate=w_gate, w_up=w_up, w_down=w_down, ln3_g=ln3_g, ln3_b=ln3_b)
    depth = w_in.shape[0]
    alpha = (2 * depth) ** 0.25
    bp, tp, d = x_prompt.shape
    bs, ts, _ = x_sample.shape
    w = GROUP_WIDTH
    n_pool, _, page = cache_kv.shape[:3]
    assert tp % TQ == 0 and ts <= SAMPLE_ROWS and (page_table.shape[1] * page) % MOBA_BLOCK == 0
    m_len = mem_prompt.shape[1]
    cache_t = jnp.moveaxis(cache_kv, 2, -1).reshape(n_pool, depth, N_KV_SLOTS * w, page)
    mem_sample_t = jnp.moveaxis(cache_mem_kv, 2, -1).reshape(bs, depth, 2 * w, m_len)

    tab_moba, tab_diff = rel_bias[:, :N_HEADS], rel_bias[:, N_HEADS:]
    pb_moba, far_moba = _prompt_bias(tab_moba)
    pb_diff, far_diff = _prompt_bias(tab_diff)
    sb_moba = _sample_bias(tab_moba, page, ts)
    sb_diff = jnp.concatenate([_sample_bias(tab_diff, page, ts)] * 2, axis=1)

    xp = x_prompt
    xs = jnp.pad(x_sample, ((0, 0), (0, SAMPLE_ROWS - ts), (0, 0)))
    st_p = jnp.zeros((bp, SUBLANES, w), F32)
    outs = dict(kv_p=[], mem_p=[], h_p=[], conv_p=[], kv_s=[], h_s=[], conv_s=[])
    for l in range(depth):
        lp = _layer_params(l, p)
        mkv_t = _memory_kv_t(mem_prompt, lp['w_xkv_t'])

        qag, kvt, kvb = _project(xp, lp['w_q'], lp['w_kv_t'])
        y_a, st = _rglru(qag, st_p, lp['pvec'], lp['w_gates'], tc=TQ, tv=TQ)
        y_b, y_c, y_d = _prompt_mixers(qag, kvt, kvb, pb_moba, far_moba, pb_diff, far_diff,
                                       lp['dl'], lp['subln'], lp['lam_init'])
        xp = _finish_layer(xp, (y_a, y_b, y_c, y_d), mkv_t, 0, lp, alpha, ROW_TILE)
        outs['kv_p'].append(kvt)
        outs['mem_p'].append(mkv_t[:, 0])
        outs['h_p'].append(st[:, CONV_WIDTH - 1])
        outs['conv_p'].append(st[:, :CONV_WIDTH - 1])

        qag, kv = _matmul(xs.reshape(bs * SAMPLE_ROWS, d), [lp['w_q'], lp['w_kv']])
        qag, kv = qag.reshape(bs, SAMPLE_ROWS, 5 * w), kv.reshape(bs, SAMPLE_ROWS, N_KV_SLOTS * w)
        y_a, st = _rglru(qag, _pack_state(state_conv[:, l], state_rglru_h[:, l]), lp['pvec'], lp['w_gates'],
                         tc=SAMPLE_ROWS, tv=ts)
        new_page = jnp.pad(jnp.swapaxes(kv, 1, 2), ((0, 0), (0, 0), (0, page - SAMPLE_ROWS)))
        y3 = _paged_attention(qag[:, :, 2 * w:], new_page, cache_t, page_table, l, sb_moba, sb_diff,
                              lp['dl'], lp['subln'], lp['lam_init'])
        xs = _finish_layer(xs, (y_a, y3[:, :, :w], y3[:, :, w:2 * w], y3[:, :, 2 * w:]), mem_sample_t, l,
                           lp, alpha, SAMPLE_ROWS)
        outs['kv_s'].append(kv[:, :ts].reshape(bs, ts, N_KV_SLOTS, N_HEADS, HEAD_DIM))
        outs['h_s'].append(st[:, CONV_WIDTH - 1])
        outs['conv_s'].append(st[:, :CONV_WIDTH - 1])

    stack = lambda k: jnp.stack(outs[k], axis=1)
    return (xp, xs[:, :ts], _token_major(stack('kv_p'), 2), _token_major(stack('mem_p'), 2), stack('h_p'),
            stack('conv_p'), stack('kv_s'), stack('h_s'), stack('conv_s'))
```

```python
import functools
import math

import numpy as np
import jax
import jax.numpy as jnp
from jax import lax
from jax.experimental import pallas as pl
from jax.experimental.pallas import tpu as pltpu

F32 = jnp.float32
BF16 = jnp.bfloat16

HEAD_DIM = 64
N_HEADS = 4
GROUP_WIDTH = N_HEADS * HEAD_DIM
N_KV_SLOTS = 6
CONV_WIDTH = 4
LRU_C = 8.0
MOBA_BLOCK = 256
MOBA_TOPK = 3
DIFF_HALF = HEAD_DIM // 2
N_BUCKETS = 32
MAX_EXACT = N_BUCKETS // 2
MAX_DISTANCE = 128
N_GROUPS = 4
EXPERTS_PER_GROUP = 4
N_EXPERTS = N_GROUPS * EXPERTS_PER_GROUP
LN_EPS = 1e-5
NEG = -1e30
LOG2E = math.log2(math.e)
SOFTPLUS_LINEAR = 64.0
STICK_DEAD = -192.0
LANES = 128
SUBLANES = 8
VMEM_LIMIT = 56 * 1024 * 1024
TQ = MOBA_BLOCK
Q_TILES = 2
EXPERTS_PER_STEP = 2
ROW_TILE = 1024
SAMPLE_ROWS = SUBLANES
PAGES_PER_STEP = 8


def _cparams(*sem):
    return pltpu.CompilerParams(dimension_semantics=sem, vmem_limit_bytes=VMEM_LIMIT)


def _softplus(x):
    return jnp.maximum(x, 0.0) + jnp.log1p(jnp.exp(-jnp.abs(x)))


def _sigmoid(x):
    return 1.0 / (1.0 + jnp.exp(-x))


def _dot(a, b, precision=None):
    return jnp.dot(a, b, preferred_element_type=F32, precision=precision)


def _dot_nt(a, b):
    return lax.dot_general(a, b, (((1,), (1,)), ((), ())), preferred_element_type=F32)


def _layer_norm(x, g, b):
    mu = jnp.mean(x, axis=-1, keepdims=True)
    xc = x - mu
    var = jnp.mean(xc * xc, axis=-1, keepdims=True)
    return xc * lax.rsqrt(var + LN_EPS) * g + b


def _bucket_table(n):
    d = np.arange(n)
    large = MAX_EXACT + (np.log(np.maximum(d, 1).astype(np.float32) / MAX_EXACT)
                         / math.log(MAX_DISTANCE / MAX_EXACT) * (N_BUCKETS - MAX_EXACT)).astype(np.int32)
    return np.where(d < MAX_EXACT, d, np.minimum(large, N_BUCKETS - 1)).astype(np.int32)


def _mm_kernel(*refs, n_out):
    x = refs[0][...].astype(BF16)
    for w_ref, o_ref in zip(refs[1:1 + n_out], refs[1 + n_out:]):
        o_ref[...] = _dot(x, w_ref[...])


def _matmul(x, ws, tm=512):
    m, k = x.shape
    tm = min(tm, m)
    return pl.pallas_call(
        functools.partial(_mm_kernel, n_out=len(ws)),
        grid=(m // tm,),
        in_specs=[pl.BlockSpec((tm, k), lambda i: (i, 0))] + [pl.BlockSpec(w.shape, lambda i: (0, 0)) for w in ws],
        out_specs=[pl.BlockSpec((tm, w.shape[1]), lambda i: (i, 0)) for w in ws],
        out_shape=[jax.ShapeDtypeStruct((m, w.shape[1]), F32) for w in ws],
        compiler_params=_cparams("parallel"),
    )(x, *ws)


def _proj_kernel(x_ref, wq_ref, wkvt_ref, q_ref, kvt_ref, kvb_ref):
    x = x_ref[0].astype(BF16)
    q_ref[0] = _dot(x, wq_ref[...])
    kvt = _dot_nt(wkvt_ref[...], x)
    kvt_ref[0] = kvt
    kvb_ref[0] = kvt.astype(BF16)


def _project(x, w_q, w_kv_t, tm=512):
    b, t, d = x.shape
    tm = min(tm, t)
    nq, nkv = w_q.shape[1], w_kv_t.shape[0]
    return pl.pallas_call(
        _proj_kernel,
        grid=(b, t // tm),
        in_specs=[pl.BlockSpec((1, tm, d), lambda i, j: (i, j, 0)),
                  pl.BlockSpec(w_q.shape, lambda i, j: (0, 0)),
                  pl.BlockSpec(w_kv_t.shape, lambda i, j: (0, 0))],
        out_specs=[pl.BlockSpec((1, tm, nq), lambda i, j: (i, j, 0)),
                   pl.BlockSpec((1, nkv, tm), lambda i, j: (i, 0, j)),
                   pl.BlockSpec((1, nkv, tm), lambda i, j: (i, 0, j))],
        out_shape=[jax.ShapeDtypeStruct((b, t, nq), F32), jax.ShapeDtypeStruct((b, nkv, t), F32),
                   jax.ShapeDtypeStruct((b, nkv, t), BF16)],
        compiler_params=_cparams("parallel", "parallel"),
    )(x, w_q, w_kv_t)


def _mem_kernel(m_ref, w_ref, o_ref):
    o_ref[0, 0] = _dot_nt(w_ref[...], m_ref[0].astype(BF16))


def _memory_kv_t(mem, w_xkv_t):
    b, m, d = mem.shape
    n = w_xkv_t.shape[0]
    return pl.pallas_call(
        _mem_kernel, grid=(b,),
        in_specs=[pl.BlockSpec((1, m, d), lambda i: (i, 0, 0)), pl.BlockSpec(w_xkv_t.shape, lambda i: (0, 0))],
        out_specs=pl.BlockSpec((1, 1, n, m), lambda i: (i, 0, 0, 0)),
        out_shape=jax.ShapeDtypeStruct((b, 1, n, m), F32),
        compiler_params=_cparams("parallel"),
    )(mem, w_xkv_t)


def _rglru_kernel(x_ref, g_ref, st_ref, pv_ref, w_ref, y_ref, so_ref, carry_ref, *, tc, tv):
    c = pl.program_id(1)

    @pl.when(c == 0)
    def _():
        carry_ref[...] = st_ref[0]

    w = GROUP_WIDTH
    x = x_ref[0]
    row = lax.broadcasted_iota(jnp.int32, (tc, w), 0)
    pv = pv_ref[...]
    conv_b, ba, bx, lam = pv[0:1], pv[1:2], pv[2:3], pv[3:4]
    prev = carry_ref[...]
    u = conv_b + pv[4 + CONV_WIDTH - 1:4 + CONV_WIDTH] * x
    for sft in range(1, CONV_WIDTH):
        xs = pltpu.roll(x, sft, 0)
        for i in range(sft):
            xs = jnp.where(row == i, prev[CONV_WIDTH - 1 + i - sft:CONV_WIDTH + i - sft], xs)
        u = u + pv[4 + CONV_WIDTH - 1 - sft:4 + CONV_WIDTH - sft] * xs
    gates = _dot(u.astype(BF16), w_ref[...])
    r = _sigmoid(gates[:, :w] + ba)
    ig = _sigmoid(gates[:, w:] + bx)
    log_a = -LRU_C * r * _softplus(-lam)
    a = jnp.exp(log_a)
    b = jnp.sqrt(-jnp.tanh(log_a) * (jnp.exp(2.0 * log_a) + 1.0)) * ig * u
    s = 1
    while s < tc:
        a_s = pltpu.roll(a, s, 0)
        b_s = pltpu.roll(b, s, 0)
        m = row >= s
        b = jnp.where(m, a * b_s + b, b)
        a = jnp.where(m, a * a_s, a)
        s *= 2
    h = b + a * prev[CONV_WIDTH - 1:CONV_WIDTH]
    gt = g_ref[0]
    gelu = 0.5 * gt * (1.0 + jnp.tanh(math.sqrt(2.0 / math.pi) * (gt + 0.044715 * (gt * gt * gt))))
    y_ref[0] = h * gelu
    carry_ref[0:CONV_WIDTH - 1] = x[tv - (CONV_WIDTH - 1):tv]
    carry_ref[CONV_WIDTH - 1:CONV_WIDTH] = h[tv - 1:tv]

    @pl.when(c == pl.num_programs(1) - 1)
    def _():
        so_ref[0] = carry_ref[...]


def _rglru(qag, state, pvec, w_gates, tc, tv):
    b, t, _ = qag.shape
    w = GROUP_WIDTH
    assert t % tc == 0 and tv >= CONV_WIDTH - 1
    return pl.pallas_call(
        functools.partial(_rglru_kernel, tc=tc, tv=tv),
        grid=(b, t // tc),
        in_specs=[pl.BlockSpec((1, tc, w), lambda i, c: (i, c, 0)),
                  pl.BlockSpec((1, tc, w), lambda i, c: (i, c, 1)),
                  pl.BlockSpec((1, SUBLANES, w), lambda i, c: (i, 0, 0)),
                  pl.BlockSpec((SUBLANES, w), lambda i, c: (0, 0)),
                  pl.BlockSpec((w, 2 * w), lambda i, c: (0, 0))],
        out_specs=[pl.BlockSpec((1, tc, w), lambda i, c: (i, c, 0)),
                   pl.BlockSpec((1, SUBLANES, w), lambda i, c: (i, 0, 0))],
        out_shape=[jax.ShapeDtypeStruct((b, t, w), F32), jax.ShapeDtypeStruct((b, SUBLANES, w), F32)],
        scratch_shapes=[pltpu.VMEM((SUBLANES, w), F32)],
        compiler_params=_cparams("parallel", "arbitrary"),
    )(qag, qag, state, pvec, w_gates)


def _tri_tables(n_super):
    qs, ks = [], []
    for qs_i in range(n_super):
        for ki in range(Q_TILES * qs_i + Q_TILES - 1, -1, -1):
            qs.append(qs_i)
            ks.append(ki)
    return jnp.asarray(np.array(qs, np.int32)), jnp.asarray(np.array(ks, np.int32))


def _wide(x):
    return jnp.concatenate([x, x], axis=1)


def _values_with(vt, h, fill):
    rowh = lax.broadcasted_iota(jnp.int32, vt.shape, 0) // HEAD_DIM
    return jnp.where(rowh == h, vt, jnp.full((), fill, vt.dtype))


def _softmax_update(j, rows, t2, m_cand, vt_aug, m_ref, acc_ref, shift=None):
    m_old = m_ref[j, rows, :]
    m_new = jnp.maximum(m_old, m_cand)
    off = m_new if shift is None else m_new - shift
    p = jnp.exp2(t2 - _wide(off))
    acc_ref[j, rows, :] = jnp.exp2(m_old - m_new) * acc_ref[j, rows, :] + _dot_nt(p.astype(BF16), vt_aug)
    m_ref[j, rows, :] = m_new


def _normalised(acc):
    return acc / pltpu.roll(acc, HEAD_DIM, 1)


def _moba_select(q, km, q_blk):
    blk = lax.broadcasted_iota(jnp.int32, q_blk.shape, 1)
    blkf = blk.astype(F32)
    gate = _dot(q, km, precision=lax.Precision.HIGHEST)
    g = jnp.where(blk < q_blk, gate, -jnp.inf)
    sel = jnp.zeros(q_blk.shape, F32)
    for _ in range(MOBA_TOPK):
        mx = jnp.max(g, axis=1, keepdims=True)
        idx = jnp.min(jnp.where(g == mx, blkf, float(LANES)), axis=1, keepdims=True)
        hit = blkf == idx
        sel = jnp.where(hit, 1.0, sel)
        g = jnp.where(hit, -jnp.inf, g)
    return jnp.where(blk < q_blk, sel, 0.0)


def _diff_lambda(dl, lam_init):
    return (jnp.exp(jnp.sum(dl[0:1] * dl[1:2], axis=1, keepdims=True))
            - jnp.exp(jnp.sum(dl[2:3] * dl[3:4], axis=1, keepdims=True)) + lam_init)


def _mix_kernel(qt_ref, kt_ref, qb_ref, qc_ref, qd_ref, kv_ref,
                ntri_ref, km_ref, bias_b_ref, far_b_ref, bias_d_ref, far_d_ref, dl_ref, sub_ref,
                ob_ref, oc_ref, od_ref,
                qmb_ref, sel_ref, mb_ref, accb_ref, qmc_ref, run_ref, accc_ref, qmd_ref, md_ref, accd_ref,
                live_ref, *, lam_init):
    t = pl.program_id(2)
    qs, ki = qt_ref[t], kt_ref[t]
    off = ki - Q_TILES * qs

    @pl.when(off == Q_TILES - 1)
    def _():
        qb = qb_ref[0]
        qb2 = qb * (HEAD_DIM ** -0.5 * LOG2E)
        qc2 = qc_ref[0] * (HEAD_DIM ** -0.5 * LOG2E)
        qd2 = qd_ref[0] * (DIFF_HALF ** -0.5 * LOG2E)
        q_blk = Q_TILES * qs + lax.broadcasted_iota(jnp.int32, (Q_TILES * TQ, LANES), 0) // TQ
        for h in range(2):
            hs = slice(h * HEAD_DIM, (h + 1) * HEAD_DIM)
            sel_ref[:, h * LANES:(h + 1) * LANES] = _moba_select(qb[:, hs], km_ref[0, hs, :], q_blk).astype(BF16)
            qmb_ref[h] = qb2[:, hs].astype(BF16)
            qmc_ref[h] = qc2[:, hs].astype(BF16)
        for j in range(4):
            qmd_ref[j] = qd2[:, j * DIFF_HALF:(j + 1) * DIFF_HALF].astype(BF16)
        mb_ref[...] = jnp.full_like(mb_ref, NEG)
        md_ref[...] = jnp.full_like(md_ref, NEG)
        accb_ref[...] = jnp.zeros_like(accb_ref)
        accc_ref[...] = jnp.zeros_like(accc_ref)
        accd_ref[...] = jnp.zeros_like(accd_ref)
        run_ref[...] = jnp.zeros_like(run_ref)
        live_ref[0] = 1


    def moba_step(rows, dist):
        kt = kv_ref[0, 0, 0]
        vt = kv_ref[0, 1, 0]
        if dist != 0:
            hit = lax.broadcasted_iota(jnp.int32, (LANES, LANES), 0) == ki
            e = jnp.where(hit, 1.0, 0.0).astype(BF16)
            z = jnp.zeros_like(e)
            onehot = jnp.concatenate([jnp.concatenate([e, z], axis=1), jnp.concatenate([z, e], axis=1)], axis=0)
            picked = _dot(sel_ref[rows, :], onehot)
        for h in range(2):
            raw = _dot(qmb_ref[h, rows, :], kt[h * HEAD_DIM:(h + 1) * HEAD_DIM])
            vt_aug = _values_with(vt, h, 1.0)
            row_pen = None if dist == 0 else jnp.where(picked[:, h * LANES:(h + 1) * LANES] > 0.5, 0.0, NEG)
            if dist is not None:
                t2 = raw + bias_b_ref[h, dist]
                m_cand = jnp.max(t2, axis=1, keepdims=True)
                m_cand = m_cand if dist == 0 else m_cand + row_pen
                _softmax_update(h, rows, t2, m_cand, vt_aug, mb_ref, accb_ref, shift=row_pen)
            else:
                shift = far_b_ref[h, 0:1] + row_pen
                m_cand = jnp.max(raw, axis=1, keepdims=True) + shift
                _softmax_update(h, rows, raw, m_cand, vt_aug, mb_ref, accb_ref, shift=shift)

    def stick_step(rows, dist):
        kt = kv_ref[0, 2, 0]
        vt = kv_ref[0, 3, 0]
        n = rows.stop - rows.start
        if dist == 0:
            r = lax.broadcasted_iota(jnp.int32, (TQ, TQ), 0)
            c = lax.broadcasted_iota(jnp.int32, (TQ, TQ), 1)
            valid = c < r
        z2s, runs, parts = [], [], []
        for h in range(2):
            z2 = _dot(qmc_ref[h, rows, :], kt[h * HEAD_DIM:(h + 1) * HEAD_DIM])
            sp2 = jnp.where(z2 > SOFTPLUS_LINEAR, z2, jnp.log2(1.0 + jnp.exp2(z2)))
            if dist == 0:
                sp2 = jnp.where(valid, sp2, 0.0)
            z2s.append(z2)
            parts.append(sp2.astype(BF16))
            runs.append(run_ref[h, rows, :] - jnp.broadcast_to(jnp.sum(sp2, axis=1, keepdims=True), (n, LANES)))
        cums = _dot(jnp.concatenate(parts, axis=0), ntri_ref[...])
        acc = accc_ref[rows, :]
        for h in range(2):
            e = jnp.exp2(z2s[h] + cums[h * n:(h + 1) * n] + _wide(run_ref[h, rows, :]))
            if dist == 0:
                e = jnp.where(valid, e, 0.0)
            acc = acc + _dot_nt(e.astype(BF16), _values_with(vt, h, 0.0))
            run_ref[h, rows, :] = runs[h]
        accc_ref[rows, :] = acc

    def diff_step(rows, dist):
        kt = kv_ref[0, 4, 0]
        vt = kv_ref[0, 5, 0]
        n = rows.stop - rows.start
        for h in range(2):
            ps, alphas = [], []
            for j in (2 * h, 2 * h + 1):
                t2 = _dot(qmd_ref[j, rows, :], kt[j * DIFF_HALF:(j + 1) * DIFF_HALF])
                if dist is not None:
                    t2 = t2 + bias_d_ref[h, dist]
                    m_cand = jnp.max(t2, axis=1, keepdims=True)
                else:
                    m_cand = jnp.max(t2, axis=1, keepdims=True) + far_d_ref[h, 0:1]
                m_old = md_ref[j, rows, :]
                m_new = jnp.maximum(m_old, m_cand)
                off = m_new if dist is not None else m_new - far_d_ref[h, 0:1]
                ps.append(jnp.exp2(t2 - _wide(off)).astype(BF16))
                alphas.append(jnp.exp2(m_old - m_new))
                md_ref[j, rows, :] = m_new
            pv = _dot_nt(jnp.concatenate(ps, axis=0), _values_with(vt, h, 1.0))
            for i, j in enumerate((2 * h, 2 * h + 1)):
                accd_ref[j, rows, :] = alphas[i] * accd_ref[j, rows, :] + pv[i * n:(i + 1) * n]

    def all_mixers(rows, dist):
        moba_step(rows, dist)
        stick_step(rows, dist)
        diff_step(rows, dist)

    def near_diagonal_step(o):
        for r in range(max(o, 0), min(o + 2, Q_TILES)):
            all_mixers(slice(r * TQ, (r + 1) * TQ), r - o)
        if o + 2 < Q_TILES:
            all_mixers(slice((o + 2) * TQ, Q_TILES * TQ), None)

    for o in range(Q_TILES - 1, -2, -1):
        pl.when(off == o)(functools.partial(near_diagonal_step, o))

    @pl.when(off <= -2)
    def _():
        rows = slice(0, Q_TILES * TQ)
        moba_step(rows, None)
        diff_step(rows, None)

        @pl.when(live_ref[0] == 1)
        def _():
            stick_step(rows, None)
            live_ref[0] = (jnp.max(run_ref[...]) >= STICK_DEAD).astype(jnp.int32)

    @pl.when(ki == 0)
    def _():
        head = lax.broadcasted_iota(jnp.int32, (Q_TILES * TQ, LANES), 1) // HEAD_DIM
        ob_ref[0] = jnp.where(head == 0, _normalised(accb_ref[0]), _normalised(accb_ref[1]))
        oc_ref[0] = accc_ref[...]
        lam = _diff_lambda(dl_ref[...], lam_init)
        o = jnp.where(head == 0,
                      _normalised(accd_ref[0]) - lam * _normalised(accd_ref[1]),
                      _normalised(accd_ref[2]) - lam * _normalised(accd_ref[3]))
        o2 = o * o
        ms = jnp.where(head == 0,
                       jnp.sum(jnp.where(head == 0, o2, 0.0), axis=1, keepdims=True),
                       jnp.sum(jnp.where(head == 1, o2, 0.0), axis=1, keepdims=True)) * (1.0 / HEAD_DIM)
        od_ref[0] = o * lax.rsqrt(ms + LN_EPS) * sub_ref[...] * (1.0 - lam_init)


def _kmean_kernel(k_ref, ones_ref, o_ref):
    o_ref[0] = _dot(k_ref[0], ones_ref[...], precision=lax.Precision.HIGHEST)


def _suffix_tri(n, sign=1.0):
    i = np.arange(n)
    return jnp.asarray(sign * (i[:, None] >= i[None, :]).astype(np.float32), dtype=BF16)


def _prompt_mixers(qag, kvt, kvb, bias_b, far_b, bias_d, far_d, dl, subln, lam_init):
    b, t, _ = qag.shape
    kv5 = kvb.reshape(b, N_KV_SLOTS, 2, LANES, t)
    nb = t // MOBA_BLOCK
    assert nb <= LANES
    blk_mean = np.zeros((t, LANES), np.float32)
    blk_mean[np.arange(t), np.arange(t) // MOBA_BLOCK] = 1.0 / MOBA_BLOCK
    kmean_t = pl.pallas_call(
        _kmean_kernel, grid=(b,),
        in_specs=[pl.BlockSpec((1, GROUP_WIDTH, t), lambda i: (i, 0, 0)),
                  pl.BlockSpec((t, LANES), lambda i: (0, 0))],
        out_specs=pl.BlockSpec((1, GROUP_WIDTH, LANES), lambda i: (i, 0, 0)),
        out_shape=jax.ShapeDtypeStruct((b, GROUP_WIDTH, LANES), F32),
        compiler_params=_cparams("parallel"),
    )(kvt, jnp.asarray(blk_mean))
    rows = Q_TILES * TQ
    assert t % rows == 0
    qt, kt = _tri_tables(t // rows)
    q_spec = lambda col: pl.BlockSpec((1, rows, LANES), lambda b, p, t, qt, kt: (b, qt[t], col + p))
    kv_spec = pl.BlockSpec((1, N_KV_SLOTS, 1, LANES, TQ), lambda b, p, t, qt, kt: (b, 0, p, 0, kt[t]))
    bias_spec = pl.BlockSpec((2, 2, TQ, TQ), lambda b, p, t, qt, kt: (p, 0, 0, 0))
    far_spec = pl.BlockSpec((2, SUBLANES, LANES), lambda b, p, t, qt, kt: (p, 0, 0))
    const2 = lambda shape: pl.BlockSpec(shape, lambda b, p, t, qt, kt: (0, 0))
    out_spec = pl.BlockSpec((1, rows, LANES), lambda b, p, t, qt, kt: (b, qt[t], p))
    out = jax.ShapeDtypeStruct((b, t, GROUP_WIDTH), F32)
    stats = lambda n: pltpu.VMEM((n, rows, LANES), F32)
    return pl.pallas_call(
        functools.partial(_mix_kernel, lam_init=lam_init),
        grid_spec=pltpu.PrefetchScalarGridSpec(
            num_scalar_prefetch=2, grid=(b, 2, qt.shape[0]),
            in_specs=[q_spec(4), q_spec(6), q_spec(8), kv_spec]
                     + [const2((TQ, TQ)), pl.BlockSpec((1, LANES, LANES), lambda b, p, t, qt, kt: (b, p, 0)),
                        bias_spec, far_spec, bias_spec, far_spec, const2(dl.shape), const2((1, LANES))],
            out_specs=[out_spec] * 3,
            scratch_shapes=[pltpu.VMEM((2, rows, HEAD_DIM), BF16), pltpu.VMEM((rows, 2 * LANES), BF16),
                            stats(2), stats(2),
                            pltpu.VMEM((2, rows, HEAD_DIM), BF16), stats(2), pltpu.VMEM((rows, LANES), F32),
                            pltpu.VMEM((4, rows, DIFF_HALF), BF16), stats(4), stats(4),
                            pltpu.SMEM((1,), jnp.int32)]),
        out_shape=[out] * 3,
        compiler_params=_cparams("parallel", "parallel", "arbitrary"),
    )(qt, kt, qag, qag, qag, kv5, _suffix_tri(TQ, -1.0), kmean_t,
      bias_b, far_b, bias_d, far_d, dl, subln[:, :LANES])


def _toeplitz(g):
    h, l = g.shape
    n = l // 2
    w = jnp.concatenate([g[:, :1], jnp.flip(g[:, 1:], axis=1)], axis=1)
    rep = jnp.tile(w, (1, n))[:, :n * (l - 1)].reshape(h, n, l - 1)
    return rep[:, :, :n]


def _prompt_bias(tab):
    n = TQ
    bucket = _bucket_table(4 * n)
    assert (bucket[n + 1:] == bucket[-1]).all()
    x = np.arange(2 * n)
    d = np.where(x < n, x, x - 2 * n)
    fvec = (tab * LOG2E).T
    g0 = jnp.where(jnp.asarray(d >= 0), fvec[:, bucket[np.maximum(d, 0)]], NEG)
    g1 = fvec[:, bucket[n + d]]
    tiles = jnp.stack([_toeplitz(g0), _toeplitz(g1)], axis=1)
    far = jnp.broadcast_to(fvec[:, bucket[-1]][:, None, None], (tab.shape[1], SUBLANES, LANES))
    return tiles, far


def _expand_rows(x4):
    x = jnp.concatenate([x4] * N_HEADS, axis=0)
    rh = lax.broadcasted_iota(jnp.int32, x.shape, 0) // SAMPLE_ROWS
    lh = lax.broadcasted_iota(jnp.int32, x.shape, 1) // HEAD_DIM
    return jnp.where(rh == lh, x, 0.0)


def _collapse_rows(x):
    rh = lax.broadcasted_iota(jnp.int32, x.shape, 0) // SAMPLE_ROWS
    lh = lax.broadcasted_iota(jnp.int32, x.shape, 1) // HEAD_DIM
    x = jnp.where(rh == lh, x, 0.0)
    out = x[0:SAMPLE_ROWS]
    for h in range(1, N_HEADS):
        out = out + x[h * SAMPLE_ROWS:(h + 1) * SAMPLE_ROWS]
    return out


def _paged_kernel(pt_ref, q_ref, new_ref, *rest, g_pages, lam_init):
    page_refs = rest[:g_pages]
    bm_ref, bd_ref, tri_ref, dl_ref, sub_ref, o_ref = rest[g_pages:g_pages + 6]
    (qb_ref, qc_ref, qd_ref, mb_m, mb_l, mb_g, mb_acc,
     run_ref, accc_ref, md_ref, ld_ref, accd_ref, live_ref) = rest[g_pages + 6:]
    s_idx = pl.program_id(1)
    n_steps = pl.num_programs(1)
    nblk = mb_m.shape[0] - 1
    w = GROUP_WIDTH
    rows = N_HEADS * SAMPLE_ROWS
    page = new_ref.shape[2]
    qq = lax.broadcasted_iota(jnp.int32, (rows, page), 0) % SAMPLE_ROWS
    kk = lax.broadcasted_iota(jnp.int32, (rows, page), 1)
    tri = tri_ref[...]

    def slot(pg, i):
        return pg[i * w:(i + 1) * w, :].astype(BF16)

    def moba_block(pages, biases, pens, n):
        ss, gsum = [], None
        for pg, bias, pen in zip(pages, biases, pens):
            raw = _dot(qb_ref[...], slot(pg, 0))
            sc = raw + bias
            ss.append(sc if pen is None else sc + pen)
            gsum = raw if gsum is None else gsum + raw
        m = ss[0].max(axis=1, keepdims=True)
        for sc in ss[1:]:
            m = jnp.maximum(m, sc.max(axis=1, keepdims=True))
        psum, acc = None, None
        for pg, sc in zip(pages, ss):
            p = jnp.exp(sc - m)
            pa = _dot_nt(p.astype(BF16), slot(pg, 1))
            psum = p if psum is None else psum + p
            acc = pa if acc is None else acc + pa
        mb_m[n] = m
        mb_l[n] = jnp.sum(psum, axis=1, keepdims=True)
        mb_acc[n] = acc
        mb_g[n] = jnp.sum(gsum, axis=1, keepdims=True)

    def stick_page(pg, valid):
        z = _dot(qc_ref[...], slot(pg, 2))
        lk = -_softplus(z)
        if valid is not None:
            lk = jnp.where(valid, lk, 0.0)
        hi = lk.astype(BF16)
        lo = (lk - hi.astype(F32)).astype(BF16)
        cum = _dot(hi, tri) + _dot(lo, tri)
        e = jnp.exp(z + cum + run_ref[...])
        if valid is not None:
            e = jnp.where(valid, e, 0.0)
        accc_ref[...] += _dot_nt(e.astype(BF16), slot(pg, 3))
        run_ref[...] += jnp.sum(lk, axis=1, keepdims=True)

    def diff_pages(pages, biases, pens):
        ss = []
        for pg, bias, pen in zip(pages, biases, pens):
            sc = _dot(qd_ref[...], slot(pg, 4)) * DIFF_HALF ** -0.5 + bias
            ss.append(sc if pen is None else sc + pen)
        mx = ss[0]
        for sc in ss[1:]:
            mx = jnp.maximum(mx, sc)
        m_old = md_ref[...]
        m_new = jnp.maximum(m_old, mx.max(axis=1, keepdims=True))
        alpha = jnp.exp(m_old - m_new)
        psum, acc = None, None
        for pg, sc in zip(pages, ss):
            p = jnp.exp(sc - m_new)
            pa = _dot_nt(p.astype(BF16), slot(pg, 5))
            psum = p if psum is None else psum + p
            acc = pa if acc is None else acc + pa
        ld_ref[...] = alpha * ld_ref[...] + jnp.sum(psum, axis=1, keepdims=True)
        accd_ref[...] = alpha * accd_ref[...] + acc
        md_ref[...] = m_new

    @pl.when(s_idx == 0)
    def _():
        q = q_ref[0]
        qb_ref[...] = (_expand_rows(q[:, 0:w]) * HEAD_DIM ** -0.5).astype(BF16)
        qc_ref[...] = (_expand_rows(q[:, w:2 * w]) * HEAD_DIM ** -0.5).astype(BF16)
        qdx = _expand_rows(q[:, 2 * w:3 * w])
        half = lax.broadcasted_iota(jnp.int32, qdx.shape, 1) // DIFF_HALF % 2
        qd_ref[0:rows] = jnp.where(half == 0, qdx, 0.0).astype(BF16)
        qd_ref[rows:2 * rows] = jnp.where(half == 1, qdx, 0.0).astype(BF16)
        run_ref[...] = jnp.zeros_like(run_ref)
        live_ref[0] = 1
        accc_ref[...] = jnp.zeros_like(accc_ref)
        md_ref[...] = jnp.full_like(md_ref, NEG)
        ld_ref[...] = jnp.zeros_like(ld_ref)
        accd_ref[...] = jnp.zeros_like(accd_ref)
        new = new_ref[0]
        causal_pen = jnp.where(kk <= qq, 0.0, NEG)
        moba_block([new], [bm_ref[0]], [causal_pen], nblk)
        stick_page(new, kk < qq)
        diff_pages([new], [bd_ref[0]], [jnp.concatenate([causal_pen, causal_pen], axis=0)])

    pages = [r[0, 0] for r in page_refs]
    bm0 = jnp.where(s_idx == 0, bm_ref[1], bm_ref[2])
    bd0 = jnp.where(s_idx == 0, bd_ref[1], bd_ref[2])
    bms = [bm0] + [bm_ref[2]] * (g_pages - 1)
    bds = [bd0] + [bd_ref[2]] * (g_pages - 1)
    per_block = MOBA_BLOCK // page
    for i in range(0, g_pages, per_block):
        n = nblk - 1 - (s_idx * g_pages + i) // per_block
        moba_block(pages[i:i + per_block], bms[i:i + per_block], [None] * per_block, n)
    diff_pages(pages, bds, [None] * g_pages)

    @pl.when(live_ref[0] == 1)
    def _():
        for pg in pages:
            stick_page(pg, None)
        live_ref[0] = (jnp.max(run_ref[...]) >= STICK_DEAD / LOG2E).astype(jnp.int32)

    @pl.when(s_idx == n_steps - 1)
    def _():
        blkf = lax.broadcasted_iota(jnp.int32, (nblk, rows, 1), 0).astype(F32)
        g = mb_g[0:nblk]
        sel = jnp.zeros((nblk, rows, 1), F32)
        for _ in range(min(MOBA_TOPK, nblk)):
            mx = jnp.max(g, axis=0, keepdims=True)
            idx = jnp.min(jnp.where(g == mx, blkf, float(nblk)), axis=0, keepdims=True)
            hit = blkf == idx
            sel = jnp.where(hit, 1.0, sel)
            g = jnp.where(hit, -jnp.inf, g)
        m_own = mb_m[nblk]
        m_tot = jnp.maximum(m_own, jnp.max(jnp.where(sel > 0.5, mb_m[0:nblk], NEG), axis=0))
        wgt = jnp.where(sel > 0.5, jnp.exp(mb_m[0:nblk] - m_tot), 0.0)
        w_own = jnp.exp(m_own - m_tot)
        l_tot = w_own * mb_l[nblk] + jnp.sum(wgt * mb_l[0:nblk], axis=0)
        a_tot = w_own * mb_acc[nblk] + jnp.sum(wgt * mb_acc[0:nblk], axis=0)
        o_ref[0, :, 0:w] = _collapse_rows(a_tot / l_tot)
        o_ref[0, :, w:2 * w] = _collapse_rows(accc_ref[...])
        lam = _diff_lambda(dl_ref[...], lam_init)
        od = (accd_ref[0:rows] / ld_ref[0:rows] - lam * (accd_ref[rows:2 * rows] / ld_ref[rows:2 * rows]))
        od = _collapse_rows(od)
        od2 = od * od
        lh = lax.broadcasted_iota(jnp.int32, od.shape, 1) // HEAD_DIM
        ms = jnp.zeros_like(od)
        for h in range(N_HEADS):
            ms = jnp.where(lh == h, jnp.sum(jnp.where(lh == h, od2, 0.0), axis=1, keepdims=True), ms)
        o_ref[0, :, 2 * w:3 * w] = od * lax.rsqrt(ms * (1.0 / HEAD_DIM) + LN_EPS) * sub_ref[...] * (1.0 - lam_init)


def _paged_attention(q3, new_page, cache_t, page_table, layer, bm, bd, dl, subln, lam_init):
    b = q3.shape[0]
    n_pages = page_table.shape[1]
    page = cache_t.shape[3]
    g_pages = min(PAGES_PER_STEP, n_pages)
    assert n_pages % g_pages == 0 and MOBA_BLOCK % page == 0 and g_pages % (MOBA_BLOCK // page) == 0
    nblk = n_pages * page // MOBA_BLOCK
    rows = N_HEADS * SAMPLE_ROWS
    w = GROUP_WIDTH

    def page_spec(i):
        return pl.BlockSpec((1, 1, N_KV_SLOTS * w, page),
                            lambda bi, s, pt: (pt[bi, n_pages - 1 - (s * g_pages + i)], layer, 0, 0))

    const2 = lambda bi, s, pt: (0, 0)
    const3 = lambda bi, s, pt: (0, 0, 0)
    return pl.pallas_call(
        functools.partial(_paged_kernel, g_pages=g_pages, lam_init=lam_init),
        grid_spec=pltpu.PrefetchScalarGridSpec(
            num_scalar_prefetch=1, grid=(b, n_pages // g_pages),
            in_specs=[pl.BlockSpec((1, SAMPLE_ROWS, 3 * w), lambda bi, s, pt: (bi, 0, 0)),
                      pl.BlockSpec((1, N_KV_SLOTS * w, page), lambda bi, s, pt: (bi, 0, 0))]
                     + [page_spec(i) for i in range(g_pages)]
                     + [pl.BlockSpec(bm.shape, const3), pl.BlockSpec(bd.shape, const3),
                        pl.BlockSpec((page, page), const2), pl.BlockSpec(dl.shape, const2),
                        pl.BlockSpec((1, w), const2)],
            out_specs=pl.BlockSpec((1, SAMPLE_ROWS, 3 * w), lambda bi, s, pt: (bi, 0, 0)),
            scratch_shapes=[pltpu.VMEM((rows, w), BF16), pltpu.VMEM((rows, w), BF16),
                            pltpu.VMEM((2 * rows, w), BF16),
                            pltpu.VMEM((nblk + 1, rows, 1), F32), pltpu.VMEM((nblk + 1, rows, 1), F32),
                            pltpu.VMEM((nblk + 1, rows, 1), F32), pltpu.VMEM((nblk + 1, rows, w), F32),
                            pltpu.VMEM((rows, 1), F32), pltpu.VMEM((rows, w), F32),
                            pltpu.VMEM((2 * rows, 1), F32), pltpu.VMEM((2 * rows, 1), F32),
                            pltpu.VMEM((2 * rows, w), F32), pltpu.SMEM((1,), jnp.int32)]),
        out_shape=jax.ShapeDtypeStruct((b, SAMPLE_ROWS, 3 * w), F32),
        compiler_params=_cparams("parallel", "arbitrary"),
    )(page_table, q3, new_page, *([cache_t] * g_pages), bm, bd, _suffix_tri(page), dl, subln)


def _sample_bias(tab, page, ts):
    bucket = _bucket_table(2 * page + SAMPLE_ROWS)
    assert (bucket[page + 1:] == bucket[-1]).all()
    q = np.minimum(np.arange(SAMPLE_ROWS), ts - 1)
    k = np.arange(page)
    idx = np.stack([bucket[np.maximum(q[:, None] - k[None, :], 0)],
                    bucket[page + q[:, None] - k[None, :]],
                    np.full((SAMPLE_ROWS, page), bucket[-1])])
    b = jnp.transpose(tab[jnp.asarray(idx)], (0, 3, 1, 2))
    return b.reshape(3, N_HEADS * SAMPLE_ROWS, page)


def _post_kernel(ya_ref, yb_ref, yc_ref, yd_ref, x_ref, wo_ref, ln1_ref, wq_ref, mem_ref, wxo_ref, ln2_ref,
                 o_ref, *, alpha):
    w = GROUP_WIDTH
    mixed = None
    for i, y_ref in enumerate((ya_ref, yb_ref, yc_ref, yd_ref)):
        part = _dot(y_ref[0].astype(BF16), wo_ref[i * w:(i + 1) * w, :])
        mixed = part if mixed is None else mixed + part
    x1 = _layer_norm(alpha * x_ref[0] + mixed, ln1_ref[0:1], ln1_ref[1:2])
    qx = _dot(x1.astype(BF16), wq_ref[...])
    kt = mem_ref[0, 0, :w, :].astype(BF16)
    vt = mem_ref[0, 0, w:, :].astype(BF16)
    head = lax.broadcasted_iota(jnp.int32, qx.shape, 1) // HEAD_DIM
    o = jnp.zeros_like(qx)
    for h in range(N_HEADS):
        s = _dot(jnp.where(head == h, qx, 0.0).astype(BF16), kt) * HEAD_DIM ** -0.5
        p = jnp.exp(s - jnp.max(s, axis=1, keepdims=True))
        p = p / jnp.sum(p, axis=1, keepdims=True)
        o = jnp.where(head == h, _dot_nt(p.astype(BF16), vt), o)
    xo = _dot(o.astype(BF16), wxo_ref[...])
    o_ref[0] = _layer_norm(alpha * x1 + xo, ln2_ref[0:1], ln2_ref[1:2])


def _post(ys, x, w_out, ln1, w_xq, mem_kv_t, layer, w_xo, ln2, alpha, tm):
    b, t, d = x.shape
    w = GROUP_WIDTH
    tm = min(tm, t)
    m_len = mem_kv_t.shape[3]
    row = lambda width: pl.BlockSpec((1, tm, width), lambda i, j: (i, j, 0))
    const = lambda shape: pl.BlockSpec(shape, lambda i, j: (0,) * len(shape))
    return pl.pallas_call(
        functools.partial(_post_kernel, alpha=alpha),
        grid=(b, t // tm),
        in_specs=[row(w)] * 4 + [row(d), const(w_out.shape), const(ln1.shape), const(w_xq.shape),
                                 pl.BlockSpec((1, 1, 2 * w, m_len), lambda i, j: (i, layer, 0, 0)),
                                 const(w_xo.shape), const(ln2.shape)],
        out_specs=row(d),
        out_shape=jax.ShapeDtypeStruct((b, t, d), F32),
        compiler_params=_cparams("parallel", "parallel"),
    )(*ys, x, w_out, ln1, w_xq, mem_kv_t, w_xo, ln2)


def _route(x, wr_ref, br_ref):
    logits = _dot(x, wr_ref[...]) + br_ref[...]
    lane = lax.broadcasted_iota(jnp.int32, logits.shape, 1)
    lanef = lane.astype(F32)
    is_grp = jnp.logical_and(lane >= N_EXPERTS, lane < N_EXPERTS + N_GROUPS)
    lg = jnp.where(is_grp, logits, -jnp.inf)
    gmax = jnp.max(lg, axis=1, keepdims=True)
    pg_sel = 1.0 / jnp.sum(jnp.exp(lg - gmax), axis=1, keepdims=True)
    gsel = jnp.min(jnp.where(lg == gmax, lanef, float(LANES)), axis=1, keepdims=True) - N_EXPERTS
    in_grp = jnp.logical_and(lane < N_EXPERTS, (lane // EXPERTS_PER_GROUP).astype(F32) == gsel)
    le = jnp.where(in_grp, logits, -jnp.inf)
    pe = jnp.exp(le - jnp.max(le, axis=1, keepdims=True))
    pe = pe / jnp.sum(pe, axis=1, keepdims=True)
    cand = jnp.where(in_grp, pe, -1.0)
    p1 = jnp.max(cand, axis=1, keepdims=True)
    i1 = jnp.min(jnp.where(cand == p1, lanef, float(LANES)), axis=1, keepdims=True)
    cand = jnp.where(lanef == i1, -1.0, cand)
    p2 = jnp.max(cand, axis=1, keepdims=True)
    i2 = jnp.min(jnp.where(cand == p2, lanef, float(LANES)), axis=1, keepdims=True)
    top = jnp.where(lanef == i1, p1, 0.0) + jnp.where(lanef == i2, p2, 0.0)
    return pg_sel * top / (p1 + p2)


def _moe_kernel(x_ref, wr_ref, br_ref, wg_ref, wu_ref, wd_ref, ln_ref, o_ref, xb_ref, cw_ref, acc_ref, *, alpha):
    e = pl.program_id(1)

    @pl.when(e == 0)
    def _():
        x = x_ref[...]
        xb_ref[...] = x.astype(BF16)
        cw_ref[...] = _route(xb_ref[...], wr_ref, br_ref)
        acc_ref[...] = jnp.zeros_like(acc_ref)

    xb = xb_ref[...]
    lane = lax.broadcasted_iota(jnp.int32, cw_ref.shape, 1)
    acc = acc_ref[...]
    for k in range(EXPERTS_PER_STEP):
        gate = _dot(xb, wg_ref[k])
        hid = gate * _sigmoid(gate) * _dot(xb, wu_ref[k])
        y = _dot(hid.astype(BF16), wd_ref[k])
        cw = jnp.sum(jnp.where(lane == e * EXPERTS_PER_STEP + k, cw_ref[...], 0.0), axis=1, keepdims=True)
        acc = acc + cw * y
    acc_ref[...] = acc

    @pl.when(e == pl.num_programs(1) - 1)
    def _():
        o_ref[...] = _layer_norm(alpha * x_ref[...] + acc_ref[...], ln_ref[0:1], ln_ref[1:2])


def _moe(x, w_router, b_router, w_gate, w_up, w_down, ln3, alpha, tm):
    n, d = x.shape
    tm = min(tm, n)
    n_e, _, d_ff = w_gate.shape
    return pl.pallas_call(
        functools.partial(_moe_kernel, alpha=alpha),
        grid=(n // tm, n_e // EXPERTS_PER_STEP),
        in_specs=[pl.BlockSpec((tm, d), lambda i, e: (i, 0)),
                  pl.BlockSpec(w_router.shape, lambda i, e: (0, 0)),
                  pl.BlockSpec(b_router.shape, lambda i, e: (0, 0)),
                  pl.BlockSpec((EXPERTS_PER_STEP, d, d_ff), lambda i, e: (e, 0, 0)),
                  pl.BlockSpec((EXPERTS_PER_STEP, d, d_ff), lambda i, e: (e, 0, 0)),
                  pl.BlockSpec((EXPERTS_PER_STEP, d_ff, d), lambda i, e: (e, 0, 0)),
                  pl.BlockSpec(ln3.shape, lambda i, e: (0, 0))],
        out_specs=pl.BlockSpec((tm, d), lambda i, e: (i, 0)),
        out_shape=jax.ShapeDtypeStruct((n, d), F32),
        scratch_shapes=[pltpu.VMEM((tm, d), BF16), pltpu.VMEM((tm, LANES), F32), pltpu.VMEM((tm, d), F32)],
        compiler_params=_cparams("parallel", "arbitrary"),
    )(x, w_router, b_router, w_gate, w_up, w_down, ln3)


def _block_diag(wh):
    h, d, _ = wh.shape
    eye = jnp.eye(h, dtype=wh.dtype)
    return (eye[:, None, :, None] * wh[:, :, None, :]).reshape(h * d, h * d)


def _layer_params(l, p):
    w = GROUP_WIDTH
    w_in = p['w_in'][l]
    cols = lambda idx: jnp.concatenate([w_in[:, i * w:(i + 1) * w] for i in idx], axis=1).astype(BF16)
    d = w_in.shape[0]
    w_router = jnp.zeros((d, LANES), F32)
    w_router = w_router.at[:, :N_EXPERTS].set(p['w_re'][l]).at[:, N_EXPERTS:N_EXPERTS + N_GROUPS].set(p['w_rg'][l])
    b_router = jnp.zeros((1, LANES), F32)
    b_router = b_router.at[0, :N_EXPERTS].set(p['b_re'][l]).at[0, N_EXPERTS:N_EXPERTS + N_GROUPS].set(p['b_rg'][l])
    w_kv = cols((3, 4, 6, 7, 9, 10))
    return dict(
        w_q=cols((0, 1, 2, 5, 8)),
        w_kv=w_kv, w_kv_t=w_kv.T,
        pvec=jnp.concatenate([p['conv_b'][l][None], p['lru_ba'][l][None], p['lru_bx'][l][None],
                              p['lru_lambda'][l][None], p['conv_w'][l]], axis=0),
        w_gates=jnp.concatenate([_block_diag(p['lru_wa'][l]), _block_diag(p['lru_wx'][l])], axis=1).astype(BF16),
        dl=p['diff_lambda'][l],
        subln=jnp.tile(p['diff_subln'][l], N_HEADS)[None],
        lam_init=0.8 - 0.6 * math.exp(-0.3 * l),
        w_out=p['w_out'][l].astype(BF16),
        ln1=jnp.stack([p['ln1_g'][l], p['ln1_b'][l]]),
        w_xq=p['w_xq'][l].astype(BF16),
        w_xo=p['w_xo'][l].astype(BF16),
        ln2=jnp.stack([p['ln2_g'][l], p['ln2_b'][l]]),
        w_router=w_router.astype(BF16), b_router=b_router,
        w_gate=p['w_gate'][l].astype(BF16), w_up=p['w_up'][l].astype(BF16), w_down=p['w_down'][l].astype(BF16),
        ln3=jnp.stack([p['ln3_g'][l], p['ln3_b'][l]]),
        w_xkv_t=p['w_xkv'][l].astype(BF16).T,
    )


def _pack_state(conv_buf, h0):
    b, _, w = conv_buf.shape
    pad = jnp.zeros((b, SUBLANES - CONV_WIDTH, w), F32)
    return jnp.concatenate([conv_buf, h0[:, None], pad], axis=1)


def _finish_layer(x, ys, mem_kv_t, layer, lp, alpha, tm):
    b, t, d = x.shape
    x2 = _post(ys, x, lp['w_out'], lp['ln1'], lp['w_xq'], mem_kv_t, layer, lp['w_xo'], lp['ln2'], alpha, tm)
    x3 = _moe(x2.reshape(b * t, d), lp['w_router'], lp['b_router'], lp['w_gate'], lp['w_up'], lp['w_down'],
              lp['ln3'], alpha, ROW_TILE)
    return x3.reshape(b, t, d)


def _token_major(x_t, lead):
    n_slots = x_t.shape[-2] // GROUP_WIDTH
    x = x_t.reshape(x_t.shape[:lead] + (n_slots, N_HEADS, HEAD_DIM, x_t.shape[-1]))
    return jnp.moveaxis(x, -1, lead)


def kernel(x_prompt, x_sample, cache_kv, cache_mem_kv, state_rglru_h, state_conv, page_table, mem_prompt, rel_bias, w_in, conv_w, conv_b, lru_wa, lru_ba, lru_wx, lru_bx, lru_lambda, diff_lambda, diff_subln, w_out, ln1_g, ln1_b, w_xq, w_xkv, w_xo, ln2_g, ln2_b, w_rg, b_rg, w_re, b_re, w_gate, w_up, w_down, ln3_g, ln3_b):
    p = dict(w_in=w_in, conv_w=conv_w, conv_b=conv_b, lru_wa=lru_wa, lru_ba=lru_ba, lru_wx=lru_wx, lru_bx=lru_bx,
             lru_lambda=lru_lambda, diff_lambda=diff_lambda, diff_subln=diff_subln, w_out=w_out, ln1_g=ln1_g,
             ln1_b=ln1_b, w_xq=w_xq, w_xkv=w_xkv, w_xo=w_xo, ln2_g=ln2_g, ln2_b=ln2_b, w_rg=w_rg, b_rg=b_rg,
             w_re=w_re, b_re=b_re, w_gate=w_gate, w_up=w_up, w_down=w_down, ln3_g=ln3_g, ln3_b=ln3_b)
    depth = w_in.shape[0]
    alpha = (2 * depth) ** 0.25
    bp, tp, d = x_prompt.shape
    bs, ts, _ = x_sample.shape
    w = GROUP_WIDTH
    n_pool, _, page = cache_kv.shape[:3]
    assert tp % TQ == 0 and ts <= SAMPLE_ROWS and (page_table.shape[1] * page) % MOBA_BLOCK == 0
    m_len = mem_prompt.shape[1]
    cache_t = jnp.moveaxis(cache_kv, 2, -1).reshape(n_pool, depth, N_KV_SLOTS * w, page)
    mem_sample_t = jnp.moveaxis(cache_mem_kv, 2, -1).reshape(bs, depth, 2 * w, m_len)

    tab_moba, tab_diff = rel_bias[:, :N_HEADS], rel_bias[:, N_HEADS:]
    pb_moba, far_moba = _prompt_bias(tab_moba)
    pb_diff, far_diff = _prompt_bias(tab_diff)
    sb_moba = _sample_bias(tab_moba, page, ts)
    sb_diff = jnp.concatenate([_sample_bias(tab_diff, page, ts)] * 2, axis=1)

    xp = x_prompt
    xs = jnp.pad(x_sample, ((0, 0), (0, SAMPLE_ROWS - ts), (0, 0)))
    st_p = jnp.zeros((bp, SUBLANES, w), F32)
    outs = dict(kv_p=[], mem_p=[], h_p=[], conv_p=[], kv_s=[], h_s=[], conv_s=[])
    for l in range(depth):
        lp = _layer_params(l, p)
        mkv_t = _memory_kv_t(mem_prompt, lp['w_xkv_t'])

        qag, kvt, kvb = _project(xp, lp['w_q'], lp['w_kv_t'])
        y_a, st = _rglru(qag, st_p, lp['pvec'], lp['w_gates'], tc=TQ, tv=TQ)
        y_b, y_c, y_d = _prompt_mixers(qag, kvt, kvb, pb_moba, far_moba, pb_diff, far_diff,
                                       lp['dl'], lp['subln'], lp['lam_init'])
        xp = _finish_layer(xp, (y_a, y_b, y_c, y_d), mkv_t, 0, lp, alpha, ROW_TILE)
        outs['kv_p'].append(kvt)
        outs['mem_p'].append(mkv_t[:, 0])
        outs['h_p'].append(st[:, CONV_WIDTH - 1])
        outs['conv_p'].append(st[:, :CONV_WIDTH - 1])

        qag, kv = _matmul(xs.reshape(bs * SAMPLE_ROWS, d), [lp['w_q'], lp['w_kv']])
        qag, kv = qag.reshape(bs, SAMPLE_ROWS, 5 * w), kv.reshape(bs, SAMPLE_ROWS, N_KV_SLOTS * w)
        y_a, st = _rglru(qag, _pack_state(state_conv[:, l], state_rglru_h[:, l]), lp['pvec'], lp['w_gates'],
                         tc=SAMPLE_ROWS, tv=ts)
        new_page = jnp.pad(jnp.swapaxes(kv, 1, 2), ((0, 0), (0, 0), (0, page - SAMPLE_ROWS)))
        y3 = _paged_attention(qag[:, :, 2 * w:], new_page, cache_t, page_table, l, sb_moba, sb_diff,
                              lp['dl'], lp['subln'], lp['lam_init'])
        xs = _finish_layer(xs, (y_a, y3[:, :, :w], y3[:, :, w:2 * w], y3[:, :, 2 * w:]), mem_sample_t, l,
                           lp, alpha, SAMPLE_ROWS)
        outs['kv_s'].append(kv[:, :ts].reshape(bs, ts, N_KV_SLOTS, N_HEADS, HEAD_DIM))
        outs['h_s'].append(st[:, CONV_WIDTH - 1])
        outs['conv_s'].append(st[:, :CONV_WIDTH - 1])

    stack = lambda k: jnp.stack(outs[k], axis=1)
    return (xp, xs[:, :ts], _token_major(stack('kv_p'), 2), _token_major(stack('mem_p'), 2), stack('h_p'),
            stack('conv_p'), stack('kv_s'), stack('h_s'), stack('conv_s'))
```
